```python
import math
import jax
import jax.numpy as jnp
from jax import lax
import numpy as np

D_MODEL = 1024
BATCH = 4
SEQ = 8192
DEPTH = 2
DEC_BATCH = 16
DEC_SEQ = 2048
PAST_LEN = 128

GRID_W = 64
WIN_R = 8
WIN_C = 16
N_HEADS = 16
HEAD_DIM = D_MODEL // N_HEADS
ATT_W = N_HEADS * HEAD_DIM
EXPAND = 2
D_INNER = EXPAND * D_MODEL
SSM_HEAD_DIM = 64
SSM_HEADS = D_INNER // SSM_HEAD_DIM
N_GROUPS = 4
D_STATE = 128
CONV_W = 5
CONV_DIM = D_INNER + 2 * N_GROUPS * D_STATE
CHUNK = 128
DT_MIN = 0.001
DT_MAX = 0.1
D_FF = 2816
IN_COLS = 3 * ATT_W + D_INNER + CONV_DIM + 2 * SSM_HEADS + 2 * D_MODEL
RMS_EPS = 1e-6

kernel_name = 'hybrid_natten_ssd_macaron_encoder'


def _in_proj_splits():
    sizes = (ATT_W, ATT_W, ATT_W, D_INNER, CONV_DIM, SSM_HEADS, SSM_HEADS, D_MODEL, D_MODEL)
    return [int(s) for s in np.cumsum(sizes)[:-1]]


def rms_norm(x, g):
    x32 = x.astype(jnp.float32)
    y = x32 * lax.rsqrt(jnp.mean(x32 * x32, axis=-1, keepdims=True) + RMS_EPS)
    return (y * g.astype(jnp.float32)).astype(x.dtype)


def swiglu(h, w_up, w_down):
    a, b = jnp.split(h @ w_up, 2, axis=-1)
    return (jax.nn.silu(a) * b) @ w_down


def neighbourhood_attention(q, k, v, rel_bias):
    b, l, h, dh = q.shape
    rows = l // GRID_W
    wr = min(WIN_R, rows)
    qg = q.reshape(b, rows, GRID_W, h, dh)
    kg = k.reshape(b, rows, GRID_W, h, dh)
    vg = v.reshape(b, rows, GRID_W, h, dh)
    cols = jnp.arange(GRID_W)
    col_start = jnp.clip(cols - WIN_C // 2, 0, GRID_W - WIN_C)
    col_valid = (cols[None, :] >= col_start[:, None]) & (cols[None, :] < col_start[:, None] + WIN_C)
    col_idx = jnp.clip(cols[None, :] - cols[:, None] + WIN_C - 1, 0, 2 * WIN_C - 2)

    def row_block(r):
        r0 = jnp.clip(r - WIN_R // 2, 0, rows - wr)
        q_r = lax.dynamic_index_in_dim(qg, r, axis=1, keepdims=False)
        k_r = lax.dynamic_slice_in_dim(kg, r0, wr, axis=1)
        v_r = lax.dynamic_slice_in_dim(vg, r0, wr, axis=1)
        row_idx = r0 + jnp.arange(wr) - r + WIN_R - 1
        bias = rel_bias[:, row_idx[None, :, None], col_idx[:, None, :]]
        s = jnp.einsum('bqhd,bikhd->bhqik', q_r, k_r, preferred_element_type=jnp.float32)
        s = jnp.where(col_valid[:, None, :], s + bias.astype(jnp.float32), -jnp.inf)
        p = jax.nn.softmax(s.reshape(b, h, GRID_W, wr * GRID_W), axis=-1).reshape(s.shape)
        return jnp.einsum('bhqik,bikhd->bqhd', p.astype(v.dtype), v_r)

    out = lax.map(row_block, jnp.arange(rows))
    return jnp.moveaxis(out, 0, 1).reshape(b, l, h * dh)


def centred_depthwise_conv(u, w, bias):
    out = lax.conv_general_dilated(
        u, w[:, None, :].astype(u.dtype), window_strides=(1,),
        padding=[(CONV_W // 2, CONV_W // 2)],
        dimension_numbers=('NWC', 'WIO', 'NWC'),
        feature_group_count=u.shape[-1])
    return out + bias.astype(u.dtype)


def ssd_chunked(x, dt, a, bm, cm):
    b, l, h, p = x.shape
    g, n = bm.shape[2], bm.shape[3]
    j = h // g
    c = l // CHUNK
    q = CHUNK
    xd = (x * dt[..., None].astype(x.dtype)).reshape(b, c, q, g, j, p)
    adt = jnp.moveaxis((dt * a).reshape(b, c, q, g, j), 2, -1)
    a_cum = jnp.cumsum(adt, axis=-1)
    bc = bm.reshape(b, c, q, g, n)
    cc = cm.reshape(b, c, q, g, n)
    causal = jnp.tril(jnp.ones((q, q), dtype=bool))
    diff = a_cum[..., :, None] - a_cum[..., None, :]
    lmat = jnp.exp(jnp.where(causal, diff, -jnp.inf)).astype(x.dtype)
    cb = jnp.einsum('bcqgn,bcsgn->bcgqs', cc, bc)
    y_diag = jnp.einsum('bcgjqs,bcsgjp->bcqgjp', cb[:, :, :, None] * lmat, xd)
    decay_s = jnp.exp(a_cum[..., -1:] - a_cum).astype(x.dtype)
    states = jnp.einsum('bcsgn,bcgjs,bcsgjp->bcgjpn', bc, decay_s, xd)
    chunk_decay = jnp.exp(a_cum[..., -1]).astype(states.dtype)

    def step(carry, inp):
        st, dec = inp
        return carry * dec[..., None, None] + st, carry

    init = jnp.zeros((b, g, j, p, n), states.dtype)
    _, prev = lax.scan(step, init, (jnp.moveaxis(states, 1, 0), jnp.moveaxis(chunk_decay, 1, 0)))
    prev = jnp.moveaxis(prev, 0, 1)
    y_off = jnp.einsum('bcqgn,bcgjpn,bcgjq->bcqgjp', cc, prev, jnp.exp(a_cum).astype(x.dtype))
    return (y_diag + y_off).reshape(b, l, h, p)


def bidir_ssd_mixer(z, xbc, dt_f_raw, dt_b_raw, conv_w, conv_b, dt_bias_f, dt_bias_b,
                    a_log_f, a_log_b, d_skip, ssm_norm):
    b, l, _ = z.shape
    xbc = jax.nn.silu(centred_depthwise_conv(xbc, conv_w, conv_b))
    xs, bm, cm = jnp.split(xbc, [D_INNER, D_INNER + N_GROUPS * D_STATE], axis=-1)
    xh = xs.reshape(b, l, SSM_HEADS, SSM_HEAD_DIM)
    bm = bm.reshape(b, l, N_GROUPS, D_STATE)
    cm = cm.reshape(b, l, N_GROUPS, D_STATE)
    dt_f = jax.nn.softplus(dt_f_raw.astype(jnp.float32) + dt_bias_f.astype(jnp.float32))
    dt_b = jax.nn.softplus(dt_b_raw.astype(jnp.float32) + dt_bias_b.astype(jnp.float32))
    a_f = -jnp.exp(a_log_f.astype(jnp.float32))
    a_b = -jnp.exp(a_log_b.astype(jnp.float32))
    y_f = ssd_chunked(xh, dt_f, a_f, bm, cm)
    flip = lambda t: jnp.flip(t, axis=1)
    y_b = flip(ssd_chunked(flip(xh), flip(dt_b), a_b, flip(bm), flip(cm)))
    y = y_f + y_b + d_skip[:, None].astype(xh.dtype) * xh
    y = y.reshape(b, l, D_INNER) * jax.nn.silu(z)
    y = rms_norm(y.reshape(b, l, N_GROUPS, D_INNER // N_GROUPS),
                 ssm_norm.reshape(N_GROUPS, D_INNER // N_GROUPS))
    return y.reshape(b, l, D_INNER)


def encoder_layer(x, ln_ffn1, w_ffn1_up, w_ffn1_down, ln_mix, w_in, q_norm, k_norm, rel_bias,
                  conv_w, conv_b, dt_bias_fwd, dt_bias_bwd, a_log_fwd, a_log_bwd, d_skip,
                  ssm_norm, w_attn_proj, w_ssm_proj, w_out, ln_ffn2, w_ffn2_up, w_ffn2_down):
    b, l, _ = x.shape
    x = x + 0.5 * swiglu(rms_norm(x, ln_ffn1), w_ffn1_up, w_ffn1_down)
    h = rms_norm(x, ln_mix)
    proj = h @ w_in
    q, k, v, z, xbc, dt_f, dt_b, g_attn, g_ssm = jnp.split(proj, _in_proj_splits(), axis=-1)
    q = rms_norm(q.reshape(b, l, N_HEADS, HEAD_DIM), q_norm) * (HEAD_DIM ** -0.5)
    k = rms_norm(k.reshape(b, l, N_HEADS, HEAD_DIM), k_norm)
    v = v.reshape(b, l, N_HEADS, HEAD_DIM)
    attn = neighbourhood_attention(q, k, v, rel_bias)
    ssm = bidir_ssd_mixer(z, xbc, dt_f, dt_b, conv_w, conv_b, dt_bias_fwd, dt_bias_bwd,
                          a_log_fwd, a_log_bwd, d_skip, ssm_norm)
    merged = jax.nn.sigmoid(g_attn) * (attn @ w_attn_proj) + jax.nn.sigmoid(g_ssm) * (ssm @ w_ssm_proj)
    x = x + merged @ w_out
    x = x + 0.5 * swiglu(rms_norm(x, ln_ffn2), w_ffn2_up, w_ffn2_down)
    return x


def setup_inputs(seed: int = 0) -> dict:
    key = jax.random.key(seed)
    ks = jax.random.split(key, 24)
    f32 = jnp.float32

    def nrm(k, shape, fan_in):
        return jax.random.normal(k, shape, f32) * (fan_in ** -0.5)

    def gain(k, shape):
        return 1.0 + 0.01 * jax.random.normal(k, shape, f32)

    def dt_bias(k):
        u = jax.random.uniform(k, (DEPTH, SSM_HEADS), f32)
        dt = jnp.exp(u * (math.log(DT_MAX) - math.log(DT_MIN)) + math.log(DT_MIN))
        return dt + jnp.log(-jnp.expm1(-dt))

    def a_log(k):
        return jnp.log(jax.random.uniform(k, (DEPTH, SSM_HEADS), f32, minval=1.0, maxval=16.0))

    return {
        'x_prompt': jax.random.normal(ks[0], (BATCH, SEQ, D_MODEL), f32),
        'x_sample': jax.random.normal(ks[1], (DEC_BATCH, DEC_SEQ, D_MODEL), f32),
        'ln_ffn1': gain(ks[2], (DEPTH, D_MODEL)),
        'w_ffn1_up': nrm(ks[3], (DEPTH, D_MODEL, 2 * D_FF), D_MODEL),
        'w_ffn1_down': nrm(ks[4], (DEPTH, D_FF, D_MODEL), D_FF),
        'ln_mix': gain(ks[5], (DEPTH, D_MODEL)),
        'w_in': nrm(ks[6], (DEPTH, D_MODEL, IN_COLS), D_MODEL),
        'q_norm': gain(ks[7], (DEPTH, HEAD_DIM)),
        'k_norm': gain(ks[8], (DEPTH, HEAD_DIM)),
        'rel_bias': 0.1 * jax.random.normal(ks[9], (DEPTH, N_HEADS, 2 * WIN_R - 1, 2 * WIN_C - 1), f32),
        'conv_w': nrm(ks[10], (DEPTH, CONV_W, CONV_DIM), CONV_W),
        'conv_b': 0.01 * jax.random.normal(ks[11], (DEPTH, CONV_DIM), f32),
        'dt_bias_fwd': dt_bias(ks[12]),
        'dt_bias_bwd': dt_bias(ks[13]),
        'a_log_fwd': a_log(ks[14]),
        'a_log_bwd': a_log(ks[15]),
        'd_skip': 1.0 + 0.1 * jax.random.normal(ks[16], (DEPTH, SSM_HEADS), f32),
        'ssm_norm': gain(ks[17], (DEPTH, D_INNER)),
        'w_attn_proj': nrm(ks[18], (DEPTH, ATT_W, D_MODEL), ATT_W),
        'w_ssm_proj': nrm(ks[19], (DEPTH, D_INNER, D_MODEL), D_INNER),
        'w_out': nrm(ks[20], (DEPTH, D_MODEL, D_MODEL), D_MODEL),
        'ln_ffn2': gain(ks[21], (DEPTH, D_MODEL)),
        'w_ffn2_up': nrm(ks[22], (DEPTH, D_MODEL, 2 * D_FF), D_MODEL),
        'w_ffn2_down': nrm(ks[23], (DEPTH, D_FF, D_MODEL), D_FF),
    }


def reference(x_prompt, x_sample, ln_ffn1, w_ffn1_up, w_ffn1_down, ln_mix, w_in, q_norm, k_norm,
              rel_bias, conv_w, conv_b, dt_bias_fwd, dt_bias_bwd, a_log_fwd, a_log_bwd, d_skip,
              ssm_norm, w_attn_proj, w_ssm_proj, w_out, ln_ffn2, w_ffn2_up, w_ffn2_down):
    params = (ln_ffn1, w_ffn1_up, w_ffn1_down, ln_mix, w_in, q_norm, k_norm, rel_bias,
              conv_w, conv_b, dt_bias_fwd, dt_bias_bwd, a_log_fwd, a_log_bwd, d_skip,
              ssm_norm, w_attn_proj, w_ssm_proj, w_out, ln_ffn2, w_ffn2_up, w_ffn2_down)

    def run_trunk(x):
        for layer in range(DEPTH):
            x = encoder_layer(x, *[w[layer] for w in params])
        return x

    y_prompt = run_trunk(x_prompt)
    y_sample = run_trunk(x_sample)
    return (y_prompt, y_sample)
```

```python
import functools

import jax
import jax.numpy as jnp
from jax import lax
from jax.experimental import pallas as pl
from jax.experimental.pallas import tpu as pltpu

F32 = jnp.float32
BF16 = jnp.bfloat16

D_MODEL = 1024
GRID_W = 64
WIN_R = 8
WIN_C = 16
N_HEADS = 16
HEAD_DIM = 64
D_INNER = 2048
SSM_HEADS = 32
SSM_HEAD_DIM = 64
N_GROUPS = 4
D_STATE = 128
CONV_W = 5
CONV_DIM = D_INNER + 2 * N_GROUPS * D_STATE
CHUNK = 128
D_FF = 2816
RMS_EPS = 1e-6
NEG_BIG = -1e30

VMEM_LIMIT_BYTES = 56 * 1024 * 1024

FFN_TM = 512
FFN_CK = 256
INP_TM = 256
CONV_TC = 256
CONV_HALO = 16
ATT_QR = 8
ATT_HG = 4
OUT_TM = 256
GROUP_W = D_INNER // N_GROUPS


def _resident(shape):
    nd = len(shape)
    return pl.BlockSpec(shape, lambda *_: (0,) * nd, pipeline_mode=pl.Buffered(1))


def _sigmoid(x):
    return 1.0 / (1.0 + jnp.exp(-x))


def _softplus(x):
    return jnp.maximum(x, 0.0) + jnp.log(1.0 + jnp.exp(-jnp.abs(x)))


def _rms_normed(x, gain):
    ms = jnp.mean(x * x, axis=-1, keepdims=True)
    return x * lax.rsqrt(ms + RMS_EPS) * gain


def _ffn_body(x_ref, g_ref, wu_ref, wd_ref, o_ref):
    x = x_ref[...]
    xn = _rms_normed(x, g_ref[...]).astype(BF16)
    acc = jnp.zeros(x.shape, F32)
    for c in range(D_FF // FFN_CK):
        ab = jnp.dot(xn, wu_ref[:, c * 2 * FFN_CK:(c + 1) * 2 * FFN_CK],
                     preferred_element_type=F32)
        a = ab[:, :FFN_CK]
        b = ab[:, FFN_CK:]
        h = (a * _sigmoid(a) * b).astype(BF16)
        acc = acc + jnp.dot(h, wd_ref[c * FFN_CK:(c + 1) * FFN_CK, :],
                            preferred_element_type=F32)
    o_ref[...] = x + 0.5 * acc


def _ffn(x2, gain, wu_r, wd):
    t = x2.shape[0]
    return pl.pallas_call(
        _ffn_body,
        grid=(t // FFN_TM,),
        in_specs=[
            pl.BlockSpec((FFN_TM, D_MODEL), lambda i: (i, 0)),
            _resident((1, D_MODEL)),
            _resident((D_MODEL, 2 * D_FF)),
            _resident((D_FF, D_MODEL)),
        ],
        out_specs=pl.BlockSpec((FFN_TM, D_MODEL), lambda i: (i, 0)),
        out_shape=jax.ShapeDtypeStruct((t, D_MODEL), F32),
        compiler_params=pltpu.CompilerParams(
            dimension_semantics=("parallel",), vmem_limit_bytes=VMEM_LIMIT_BYTES),
        name="ffn",
    )(x2, gain, wu_r, wd)


def _inproj_body(x_ref, g_ref, w_ref, wdt_ref, wdtT_ref, qg_ref, kg_ref, p_ref,
                 q_ref, k_ref, v_ref, z_ref, xbc_ref, gate_ref, dt_ref, dtT_ref):
    h = _rms_normed(x_ref[...], g_ref[...]).astype(BF16)

    def proj(lo, hi):
        return jnp.dot(h, w_ref[:, lo:hi], preferred_element_type=F32)

    def head_normed(r, gain):
        ms = jnp.dot((r * r).astype(BF16), p_ref[...], preferred_element_type=F32)
        return (r * lax.rsqrt(ms + RMS_EPS) * gain).astype(BF16)

    q_ref[...] = head_normed(proj(0, 1024), qg_ref[...])
    k_ref[...] = head_normed(proj(1024, 2048), kg_ref[...])
    v_ref[...] = proj(2048, 3072).astype(BF16)
    for c in range(2):
        z_ref[:, c * 1024:(c + 1) * 1024] = proj(3072 + c * 1024, 4096 + c * 1024).astype(BF16)
    for c in range(3):
        xbc_ref[:, c * 1024:(c + 1) * 1024] = proj(5120 + c * 1024, 6144 + c * 1024).astype(BF16)
    for c in range(2):
        gate_ref[:, c * 1024:(c + 1) * 1024] = proj(8192 + c * 1024, 9216 + c * 1024).astype(BF16)
    dt_ref[...] = jnp.dot(h, wdt_ref[...], preferred_element_type=F32)
    dtT_ref[...] = lax.dot_general(wdtT_ref[...], h, (((1,), (1,)), ((), ())),
                                   preferred_element_type=F32)


def _inproj(x2, gain, w_main, w_dt, w_dtT, qg, kg, pmat):
    t = x2.shape[0]
    tm = INP_TM
    row = lambda w: pl.BlockSpec((tm, w), lambda i: (i, 0))
    return pl.pallas_call(
        _inproj_body,
        grid=(t // tm,),
        in_specs=[
            row(D_MODEL),
            _resident((1, D_MODEL)),
            _resident(w_main.shape),
            _resident(w_dt.shape),
            _resident(w_dtT.shape),
            _resident((1, D_MODEL)),
            _resident((1, D_MODEL)),
            _resident((D_MODEL, D_MODEL)),
        ],
        out_specs=[row(1024), row(1024), row(1024), row(2048), row(3072), row(2048), row(128),
                   pl.BlockSpec((2 * SSM_HEADS, tm), lambda i: (0, i))],
        out_shape=[
            jax.ShapeDtypeStruct((t, 1024), BF16),
            jax.ShapeDtypeStruct((t, 1024), BF16),
            jax.ShapeDtypeStruct((t, 1024), BF16),
            jax.ShapeDtypeStruct((t, 2048), BF16),
            jax.ShapeDtypeStruct((t, 3072), BF16),
            jax.ShapeDtypeStruct((t, 2048), BF16),
            jax.ShapeDtypeStruct((t, 128), F32),
            jax.ShapeDtypeStruct((2 * SSM_HEADS, t), F32),
        ],
        compiler_params=pltpu.CompilerParams(
            dimension_semantics=("parallel",), vmem_limit_bytes=VMEM_LIMIT_BYTES),
        name="inproj",
    )(x2, gain, w_main, w_dt, w_dtT, qg, kg, pmat)


def _conv_body(prev_ref, cur_ref, next_ref, w_ref, b_ref, o_ref, ext_ref):
    i = pl.program_id(1)
    n = pl.num_programs(1)
    tc = cur_ref.shape[1]
    has_prev = (i > 0).astype(F32)
    has_next = (i < n - 1).astype(F32)
    ext_ref[0:CONV_HALO, :] = prev_ref[0].astype(F32) * has_prev
    ext_ref[CONV_HALO:CONV_HALO + tc, :] = cur_ref[0].astype(F32)
    ext_ref[CONV_HALO + tc:, :] = next_ref[0].astype(F32) * has_next
    cw = 512
    base = CONV_HALO - CONV_W // 2
    for c in range(CONV_DIM // cw):
        cols = slice(c * cw, (c + 1) * cw)
        acc = ext_ref[base:base + tc, cols] * w_ref[0:1, cols]
        for j in range(1, CONV_W):
            acc = acc + ext_ref[base + j:base + j + tc, cols] * w_ref[j:j + 1, cols]
        acc = acc + b_ref[:, cols]
        o_ref[0, :, cols] = (acc * _sigmoid(acc)).astype(BF16)


def _conv(xbc3, conv_w, conv_b):
    b, l, _ = xbc3.shape
    tc = CONV_TC
    hb = tc // CONV_HALO
    n_halo = l // CONV_HALO
    return pl.pallas_call(
        _conv_body,
        grid=(b, l // tc),
        in_specs=[
            pl.BlockSpec((1, CONV_HALO, CONV_DIM),
                         lambda bi, i: (bi, jnp.maximum(i * hb - 1, 0), 0)),
            pl.BlockSpec((1, tc, CONV_DIM), lambda bi, i: (bi, i, 0)),
            pl.BlockSpec((1, CONV_HALO, CONV_DIM),
                         lambda bi, i: (bi, jnp.minimum((i + 1) * hb, n_halo - 1), 0)),
            _resident((8, CONV_DIM)),
            _resident((1, CONV_DIM)),
        ],
        out_specs=pl.BlockSpec((1, tc, CONV_DIM), lambda bi, i: (bi, i, 0)),
        out_shape=jax.ShapeDtypeStruct((b, l, CONV_DIM), BF16),
        scratch_shapes=[pltpu.VMEM((tc + 2 * CONV_HALO, CONV_DIM), F32)],
        compiler_params=pltpu.CompilerParams(
            dimension_semantics=("parallel", "parallel"), vmem_limit_bytes=VMEM_LIMIT_BYTES),
        name="conv",
    )(xbc3, xbc3, xbc3, conv_w, conv_b)


def _attn_body(q_ref, k_ref, v_ref, bias_ref, o_ref, *, rows):
    j = pl.program_id(2)
    lane = lax.broadcasted_iota(jnp.int32, (GRID_W, 128), 1)
    low_half = lane < HEAD_DIM
    nk = WIN_R * GRID_W

    def row_body(i, carry):
        r = j * ATT_QR + i
        r0 = jnp.clip(r - WIN_R // 2, 0, rows - WIN_R)
        cls = r - r0
        qs = pl.multiple_of(i * GRID_W, GRID_W)
        ks = pl.multiple_of(r0 * GRID_W, GRID_W)
        for pair in range(ATT_HG // 2):
            cs = slice(pair * 128, (pair + 1) * 128)
            qp = q_ref[0, pl.ds(qs, GRID_W), cs]
            kp = k_ref[0, pl.ds(ks, nk), cs]
            vp = v_ref[0, pl.ds(ks, nk), cs]
            outs = []
            for hh in range(2):
                qm = jnp.where(low_half if hh == 0 else jnp.logical_not(low_half),
                               qp, jnp.zeros_like(qp))
                s = lax.dot_general(qm, kp, (((1,), (1,)), ((), ())),
                                    preferred_element_type=F32)
                s = s + bias_ref[cls, pair * 2 + hh]
                m = jnp.max(s, axis=-1, keepdims=True)
                p = jnp.exp(s - m)
                denom = jnp.sum(p, axis=-1, keepdims=True)
                o = jnp.dot(p.astype(BF16), vp, preferred_element_type=F32)
                outs.append(o / denom)
            o_ref[0, pl.ds(qs, GRID_W), cs] = jnp.where(low_half, outs[0], outs[1]).astype(BF16)
        return carry

    lax.fori_loop(0, ATT_QR, row_body, 0)


def _attention(q3, k3, v3, bias_tab):
    b, l, _ = q3.shape
    rows = l // GRID_W
    assert rows >= WIN_R and rows % ATT_QR == 0
    tq = ATT_QR * GRID_W
    cw = ATT_HG * HEAD_DIM
    return pl.pallas_call(
        functools.partial(_attn_body, rows=rows),
        grid=(b, N_HEADS // ATT_HG, rows // ATT_QR),
        in_specs=[
            pl.BlockSpec((1, tq, cw), lambda bi, g, j: (bi, j, g)),
            pl.BlockSpec((1, l, cw), lambda bi, g, j: (bi, 0, g)),
            pl.BlockSpec((1, l, cw), lambda bi, g, j: (bi, 0, g)),
            pl.BlockSpec((WIN_R, ATT_HG, GRID_W, WIN_R * GRID_W), lambda bi, g, j: (0, g, 0, 0)),
        ],
        out_specs=pl.BlockSpec((1, tq, cw), lambda bi, g, j: (bi, j, g)),
        out_shape=jax.ShapeDtypeStruct((b, l, N_HEADS * HEAD_DIM), BF16),
        compiler_params=pltpu.CompilerParams(
            dimension_semantics=("parallel", "parallel", "arbitrary"),
            vmem_limit_bytes=VMEM_LIMIT_BYTES),
        name="natten",
    )(q3, k3, v3, bias_tab)


def _attn_bias_table(rel_bias):
    cols = jnp.arange(GRID_W)
    col_start = jnp.clip(cols - WIN_C // 2, 0, GRID_W - WIN_C)
    col_valid = (cols[None, :] >= col_start[:, None]) & (cols[None, :] < col_start[:, None] + WIN_C)
    col_idx = jnp.clip(cols[None, :] - cols[:, None] + WIN_C - 1, 0, 2 * WIN_C - 2)
    cls = jnp.arange(WIN_R)[:, None]
    win = jnp.arange(WIN_R)[None, :]
    row_idx = win - cls + WIN_R - 1
    tab = rel_bias[:, row_idx[:, :, None, None], col_idx[None, None, :, :]]
    tab = jnp.where(col_valid[None, None, None], tab.astype(F32), NEG_BIG)
    tab = jnp.transpose(tab, (1, 0, 3, 2, 4))
    return tab.reshape(WIN_R, N_HEADS, GRID_W, WIN_R * GRID_W)


def _split2(v):
    hi = v.astype(BF16)
    mid = (v - hi.astype(F32)).astype(BF16)
    return hi, mid


def _split3(v):
    hi = v.astype(BF16)
    r1 = v - hi.astype(F32)
    mid = r1.astype(BF16)
    lo = (r1 - mid.astype(F32)).astype(BF16)
    return hi, mid, lo


def _ssd_body(xf_ref, xb_ref, dtf_ref, dtb_ref, dtTf_ref, dtTb_ref, bias2_ref, alog2_ref,
              biasT_ref, alogT_ref, e2_ref, yf_ref, yb_ref, s_ref):
    i = pl.program_id(1)

    @pl.when(i == 0)
    def _():
        s_ref[...] = jnp.zeros(s_ref.shape, F32)

    q = CHUNK
    ri = lax.broadcasted_iota(jnp.int32, (q, q), 0)
    ci = lax.broadcasted_iota(jnp.int32, (q, q), 1)
    lower = ri >= ci
    upper = ri <= ci
    lower_b = lower.astype(BF16)
    upper_b = upper.astype(BF16)

    lane = lax.broadcasted_iota(jnp.int32, (1, 128), 1)
    fwd_col = (lane % 64) < SSM_HEADS
    dt = _softplus(jnp.where(fwd_col, dtf_ref[0], dtb_ref[0]) + bias2_ref[...])
    adt = dt * (-jnp.exp(alog2_ref[...]))
    pieces = jnp.concatenate(_split3(adt), axis=0)
    acum = jnp.where(
        fwd_col,
        jnp.dot(jnp.concatenate([lower_b] * 3, axis=1), pieces, preferred_element_type=F32),
        jnp.dot(jnp.concatenate([upper_b] * 3, axis=1), pieces, preferred_element_type=F32))
    last = jnp.where(fwd_col, acum[q - 1:q, :], acum[0:1, :])
    fac = jnp.where(lane < 64, jnp.exp(acum), dt * jnp.exp(last - acum))
    fac_pieces = jnp.concatenate(_split2(fac), axis=1)

    def expand(col0):
        return jnp.dot(fac_pieces, e2_ref[:, col0:col0 + GROUP_W], preferred_element_type=F32)

    rowi = lax.broadcasted_iota(jnp.int32, (2 * SSM_HEADS, 1), 0)
    fwd_row = rowi < SSM_HEADS
    dt_t = _softplus(jnp.where(fwd_row, dtTf_ref[...], dtTb_ref[...]) + biasT_ref[...])
    adt_t = dt_t * (-jnp.exp(alogT_ref[...]))
    pieces_t = jnp.concatenate(_split3(adt_t), axis=1)
    acum_t = jnp.where(
        fwd_row,
        jnp.dot(pieces_t, jnp.concatenate([upper_b] * 3, axis=0), preferred_element_type=F32),
        jnp.dot(pieces_t, jnp.concatenate([lower_b] * 3, axis=0), preferred_element_type=F32))
    g_t = acum_t - jnp.log(dt_t)

    bd_r = lax.broadcasted_iota(jnp.int32, (4 * q, 4 * SSM_HEAD_DIM), 0) // q
    bd_c = lax.broadcasted_iota(jnp.int32, (4 * q, 4 * SSM_HEAD_DIM), 1) // SSM_HEAD_DIM
    bd_mask = bd_r == bd_c

    for d, (x_ref, y_ref, causal) in enumerate(((xf_ref, yf_ref, lower), (xb_ref, yb_ref, upper))):
        for g in range(N_GROUPS):
            xs_g = x_ref[0, :, g * GROUP_W:(g + 1) * GROUP_W]
            b_g = x_ref[0, :, D_INNER + g * D_STATE:D_INNER + (g + 1) * D_STATE]
            c_g = x_ref[0, :, D_INNER + (N_GROUPS + g) * D_STATE:
                        D_INNER + (N_GROUPS + g + 1) * D_STATE]
            cb = lax.dot_general(c_g, b_g, (((1,), (1,)), ((), ())),
                                 preferred_element_type=F32)
            halves = []
            for half in range(2):
                ms = []
                for hh in range(4):
                    col = d * SSM_HEADS + g * 8 + half * 4 + hh
                    diff = acum[:, col:col + 1] - g_t[col:col + 1, :]
                    ms.append((cb * jnp.exp(jnp.where(causal, diff, NEG_BIG))).astype(BF16))
                lhs = jnp.concatenate(ms, axis=1)
                x4 = xs_g[:, half * 256:(half + 1) * 256]
                bd = jnp.where(bd_mask, jnp.concatenate([x4] * 4, axis=0), jnp.zeros((), BF16))
                halves.append(jnp.dot(lhs, bd, preferred_element_type=F32))
            y_diag = jnp.concatenate(halves, axis=1)
            s_prev = s_ref[d, g]
            ea_g = expand(d * D_INNER + g * GROUP_W)
            y_off = jnp.dot(c_g, s_prev.astype(BF16), preferred_element_type=F32) * ea_g
            y_ref[0, :, g * GROUP_W:(g + 1) * GROUP_W] = (y_diag + y_off).astype(BF16)

            w_g = expand(2 * D_INNER + d * D_INNER + g * GROUP_W)
            xw = (xs_g.astype(F32) * w_g).astype(BF16)
            b_t = jnp.transpose(b_g.astype(F32)).astype(BF16)
            decay = ea_g[q - 1:q, :] if d == 0 else ea_g[0:1, :]
            s_ref[d, g] = s_prev * decay + jnp.dot(b_t, xw, preferred_element_type=F32)


def _ssd(xsbc3, dt3, dtT, bias2, alog2, biasT, alogT, e2):
    b, l, _ = xsbc3.shape
    nc = l // CHUNK
    fwd = lambda bi, i: (bi, i, 0)
    bwd = lambda bi, i: (bi, nc - 1 - i, 0)
    return pl.pallas_call(
        _ssd_body,
        grid=(b, nc),
        in_specs=[
            pl.BlockSpec((1, CHUNK, CONV_DIM), fwd),
            pl.BlockSpec((1, CHUNK, CONV_DIM), bwd),
            pl.BlockSpec((1, CHUNK, 128), fwd),
            pl.BlockSpec((1, CHUNK, 128), bwd),
            pl.BlockSpec((2 * SSM_HEADS, CHUNK), lambda bi, i: (0, bi * nc + i)),
            pl.BlockSpec((2 * SSM_HEADS, CHUNK), lambda bi, i: (0, bi * nc + nc - 1 - i)),
            _resident((1, 128)),
            _resident((1, 128)),
            _resident((2 * SSM_HEADS, 1)),
            _resident((2 * SSM_HEADS, 1)),
            _resident(e2.shape),
        ],
        out_specs=[pl.BlockSpec((1, CHUNK, D_INNER), fwd),
                   pl.BlockSpec((1, CHUNK, D_INNER), bwd)],
        out_shape=[jax.ShapeDtypeStruct((b, l, D_INNER), BF16),
                   jax.ShapeDtypeStruct((b, l, D_INNER), BF16)],
        scratch_shapes=[pltpu.VMEM((2, N_GROUPS, D_STATE, GROUP_W), F32)],
        compiler_params=pltpu.CompilerParams(
            dimension_semantics=("parallel", "arbitrary"), vmem_limit_bytes=VMEM_LIMIT_BYTES),
        name="ssd",
    )(xsbc3, xsbc3, dt3, dt3, dtT, dtT, bias2, alog2, biasT, alogT, e2)


def _out_body(x_ref, attn_ref, yf_ref, yb_ref, xs_ref, z_ref, gate_ref, dsk_ref, nrm_ref,
              wap_ref, wsp_ref, wo_ref, o_ref):
    z = z_ref[...].astype(F32)
    y = (yf_ref[...].astype(F32) + yb_ref[...].astype(F32)
         + dsk_ref[...] * xs_ref[...].astype(F32)) * (z * _sigmoid(z))
    parts = []
    for g in range(N_GROUPS):
        cols = slice(g * GROUP_W, (g + 1) * GROUP_W)
        parts.append(_rms_normed(y[:, cols], nrm_ref[:, cols]).astype(BF16))
    ssm = jnp.concatenate(parts, axis=1)
    g_attn = gate_ref[:, :D_MODEL].astype(F32)
    g_ssm = gate_ref[:, D_MODEL:].astype(F32)
    merged = (_sigmoid(g_attn) * jnp.dot(attn_ref[...], wap_ref[...], preferred_element_type=F32)
              + _sigmoid(g_ssm) * jnp.dot(ssm, wsp_ref[...], preferred_element_type=F32))
    o_ref[...] = x_ref[...] + jnp.dot(merged.astype(BF16), wo_ref[...],
                                      preferred_element_type=F32)


def _merge_out(x2, attn2, yf2, yb2, xsbc2, z2, gate2, dsk, nrm, wap, wsp, wo):
    t = x2.shape[0]
    tm = OUT_TM
    row = lambda w: pl.BlockSpec((tm, w), lambda i: (i, 0))
    return pl.pallas_call(
        _out_body,
        grid=(t // tm,),
        in_specs=[
            row(D_MODEL), row(1024), row(D_INNER), row(D_INNER), row(D_INNER), row(D_INNER),
            row(2 * D_MODEL),
            _resident((1, D_INNER)), _resident((1, D_INNER)),
            _resident(wap.shape), _resident(wsp.shape), _resident(wo.shape),
        ],
        out_specs=row(D_MODEL),
        out_shape=jax.ShapeDtypeStruct((t, D_MODEL), F32),
        compiler_params=pltpu.CompilerParams(
            dimension_semantics=("parallel",), vmem_limit_bytes=VMEM_LIMIT_BYTES),
        name="merge_out",
    )(x2, attn2, yf2, yb2, xsbc2, z2, gate2, dsk, nrm, wap, wsp, wo)


def _ffn_weights(w_up, w_down):
    nck = D_FF // FFN_CK
    a = w_up[:, :D_FF].reshape(D_MODEL, nck, FFN_CK)
    b = w_up[:, D_FF:].reshape(D_MODEL, nck, FFN_CK)
    wu_r = jnp.concatenate([a, b], axis=-1).reshape(D_MODEL, 2 * D_FF).astype(BF16)
    return wu_r, w_down.astype(BF16)


def _expansion_matrix():
    r = jnp.arange(128)[:, None]
    c = jnp.arange(128 * SSM_HEAD_DIM)[None, :] // SSM_HEAD_DIM
    e = (r == c).astype(BF16)
    return jnp.concatenate([e, e], axis=0)


def _layer_weights(layer, ln_ffn1, w_ffn1_up, w_ffn1_down, ln_mix, w_in, q_norm, k_norm, rel_bias,
                   conv_w, conv_b, dt_bias_fwd, dt_bias_bwd, a_log_fwd, a_log_bwd, d_skip,
                   ssm_norm, w_attn_proj, w_ssm_proj, w_out, ln_ffn2, w_ffn2_up, w_ffn2_down):
    w = {}
    w["ln1"] = ln_ffn1[layer].reshape(1, D_MODEL)
    w["ffn1"] = _ffn_weights(w_ffn1_up[layer], w_ffn1_down[layer])
    w["ln2"] = ln_ffn2[layer].reshape(1, D_MODEL)
    w["ffn2"] = _ffn_weights(w_ffn2_up[layer], w_ffn2_down[layer])
    w["ln_mix"] = ln_mix[layer].reshape(1, D_MODEL)
    wi = w_in[layer]
    dt0 = 3 * D_MODEL + D_INNER + CONV_DIM
    w["w_main"] = jnp.concatenate([wi[:, :dt0], wi[:, dt0 + 2 * SSM_HEADS:]], axis=1).astype(BF16)
    wdt = wi[:, dt0:dt0 + 2 * SSM_HEADS]
    w["w_dt"] = jnp.concatenate([wdt, wdt], axis=1).astype(BF16)
    w["w_dtT"] = jnp.transpose(wdt).astype(BF16)
    w["qg"] = (jnp.tile(q_norm[layer], N_HEADS) * (HEAD_DIM ** -0.5)).reshape(1, D_MODEL)
    w["kg"] = jnp.tile(k_norm[layer], N_HEADS).reshape(1, D_MODEL)
    w["bias_tab"] = _attn_bias_table(rel_bias[layer])
    w["conv_w"] = jnp.concatenate(
        [conv_w[layer], jnp.zeros((8 - CONV_W, CONV_DIM), F32)], axis=0)
    w["conv_b"] = conv_b[layer].reshape(1, CONV_DIM)
    bias = jnp.concatenate([dt_bias_fwd[layer], dt_bias_bwd[layer]])
    alog = jnp.concatenate([a_log_fwd[layer], a_log_bwd[layer]])
    w["bias2"] = jnp.tile(bias, 2).reshape(1, 128)
    w["alog2"] = jnp.tile(alog, 2).reshape(1, 128)
    w["biasT"] = bias.reshape(2 * SSM_HEADS, 1)
    w["alogT"] = alog.reshape(2 * SSM_HEADS, 1)
    w["dsk"] = jnp.repeat(d_skip[layer], SSM_HEAD_DIM).reshape(1, D_INNER)
    w["nrm"] = ssm_norm[layer].reshape(1, D_INNER)
    w["wap"] = w_attn_proj[layer].astype(BF16)
    w["wsp"] = w_ssm_proj[layer].astype(BF16)
    w["wo"] = w_out[layer].astype(BF16)
    return w


def _head_mean_matrix():
    r = jnp.arange(D_MODEL)[:, None] // HEAD_DIM
    c = jnp.arange(D_MODEL)[None, :] // HEAD_DIM
    return jnp.where(r == c, 1.0 / HEAD_DIM, 0.0).astype(BF16)


def _encoder_layer(x2, b, l, w, pmat, e2):
    t = b * l
    x2 = _ffn(x2, w["ln1"], *w["ffn1"])
    q, k, v, z, xbc, gate, dt, dt_t = _inproj(
        x2, w["ln_mix"], w["w_main"], w["w_dt"], w["w_dtT"], w["qg"], w["kg"], pmat)
    xsbc = _conv(xbc.reshape(b, l, CONV_DIM), w["conv_w"], w["conv_b"])
    attn = _attention(q.reshape(b, l, -1), k.reshape(b, l, -1), v.reshape(b, l, -1),
                      w["bias_tab"])
    yf, yb = _ssd(xsbc, dt.reshape(b, l, 128), dt_t, w["bias2"], w["alog2"], w["biasT"],
                  w["alogT"], e2)
    x2 = _merge_out(x2, attn.reshape(t, -1), yf.reshape(t, D_INNER), yb.reshape(t, D_INNER),
                    xsbc.reshape(t, CONV_DIM), z, gate, w["dsk"], w["nrm"],
                    w["wap"], w["wsp"], w["wo"])
    return _ffn(x2, w["ln2"], *w["ffn2"])


def kernel(x_prompt, x_sample, ln_ffn1, w_ffn1_up, w_ffn1_down, ln_mix, w_in, q_norm, k_norm,
           rel_bias, conv_w, conv_b, dt_bias_fwd, dt_bias_bwd, a_log_fwd, a_log_bwd, d_skip,
           ssm_norm, w_attn_proj, w_ssm_proj, w_out, ln_ffn2, w_ffn2_up, w_ffn2_down):
    params = (ln_ffn1, w_ffn1_up, w_ffn1_down, ln_mix, w_in, q_norm, k_norm, rel_bias,
              conv_w, conv_b, dt_bias_fwd, dt_bias_bwd, a_log_fwd, a_log_bwd, d_skip,
              ssm_norm, w_attn_proj, w_ssm_proj, w_out, ln_ffn2, w_ffn2_up, w_ffn2_down)
    depth = ln_ffn1.shape[0]
    layers = [_layer_weights(i, *params) for i in range(depth)]
    pmat = _head_mean_matrix()
    e2 = _expansion_matrix()

    def run_trunk(x):
        b, l, d = x.shape
        x2 = x.reshape(b * l, d)
        for w in layers:
            x2 = _encoder_layer(x2, b, l, w, pmat, e2)
        return x2.reshape(b, l, d)

    return (run_trunk(x_prompt), run_trunk(x_sample))
```

```python
import functools

import jax
import jax.numpy as jnp
import numpy as np
from jax import lax
from jax.experimental import pallas as pl
from jax.experimental.pallas import tpu as pltpu

F32 = jnp.float32
BF16 = jnp.bfloat16

D_MODEL = 1024
GRID_W = 64
WIN_R = 8
WIN_C = 16
N_HEADS = 16
HEAD_DIM = 64
D_INNER = 2048
SSM_HEADS = 32
SSM_HEAD_DIM = 64
N_GROUPS = 4
D_STATE = 128
CONV_W = 5
CONV_DIM = D_INNER + 2 * N_GROUPS * D_STATE
CHUNK = 128
D_FF = 2816
RMS_EPS = 1e-6
NEG_BIG = -1e30

VMEM_LIMIT_BYTES = 56 * 1024 * 1024

FFN_TM = 512
FFN_CK = 256
INP_TM = 256
CONV_TC = 256
CONV_HALO = 16
ATT_QR = 8
ATT_HG = 4
OUT_TM = 256
GROUP_W = D_INNER // N_GROUPS


def _resident(shape):
    nd = len(shape)
    return pl.BlockSpec(shape, lambda *_: (0,) * nd, pipeline_mode=pl.Buffered(1))


def _sigmoid(x):
    return 1.0 / (1.0 + jnp.exp(-x))


def _softplus(x):
    return jnp.maximum(x, 0.0) + jnp.log(1.0 + jnp.exp(-jnp.abs(x)))


def _rms_normed(x, gain):
    ms = jnp.mean(x * x, axis=-1, keepdims=True)
    return x * lax.rsqrt(ms + RMS_EPS) * gain


def _ffn_body(x_ref, g_ref, wu_ref, wd_ref, o_ref):
    x = x_ref[...]
    xn = _rms_normed(x, g_ref[...]).astype(BF16)
    acc = jnp.zeros(x.shape, F32)
    for c in range(D_FF // FFN_CK):
        ab = jnp.dot(xn, wu_ref[:, c * 2 * FFN_CK:(c + 1) * 2 * FFN_CK],
                     preferred_element_type=F32)
        a = ab[:, :FFN_CK]
        b = ab[:, FFN_CK:]
        h = (a * _sigmoid(a) * b).astype(BF16)
        acc = acc + jnp.dot(h, wd_ref[c * FFN_CK:(c + 1) * FFN_CK, :],
                            preferred_element_type=F32)
    o_ref[...] = x + 0.5 * acc


def _ffn(x2, gain, wu_r, wd):
    t = x2.shape[0]
    return pl.pallas_call(
        _ffn_body,
        grid=(t // FFN_TM,),
        in_specs=[
            pl.BlockSpec((FFN_TM, D_MODEL), lambda i: (i, 0)),
            _resident((1, D_MODEL)),
            _resident((D_MODEL, 2 * D_FF)),
            _resident((D_FF, D_MODEL)),
        ],
        out_specs=pl.BlockSpec((FFN_TM, D_MODEL), lambda i: (i, 0)),
        out_shape=jax.ShapeDtypeStruct((t, D_MODEL), F32),
        compiler_params=pltpu.CompilerParams(
            dimension_semantics=("parallel",), vmem_limit_bytes=VMEM_LIMIT_BYTES),
        name="ffn",
    )(x2, gain, wu_r, wd)


def _inproj_body(x_ref, g_ref, w_ref, wdt_ref, wdtT_ref, qg_ref, kg_ref, p_ref,
                 q_ref, k_ref, v_ref, z_ref, xbc_ref, gate_ref, dt_ref, dtT_ref):
    h = _rms_normed(x_ref[...], g_ref[...]).astype(BF16)

    def proj(lo, hi):
        return jnp.dot(h, w_ref[:, lo:hi], preferred_element_type=F32)

    def head_normed(r, gain):
        ms = jnp.dot((r * r).astype(BF16), p_ref[...], preferred_element_type=F32)
        return (r * lax.rsqrt(ms + RMS_EPS) * gain).astype(BF16)

    q_ref[...] = head_normed(proj(0, 1024), qg_ref[...])
    k_ref[...] = head_normed(proj(1024, 2048), kg_ref[...])
    v_ref[...] = proj(2048, 3072).astype(BF16)
    for c in range(2):
        z_ref[:, c * 1024:(c + 1) * 1024] = proj(3072 + c * 1024, 4096 + c * 1024).astype(BF16)
    for c in range(3):
        xbc_ref[:, c * 1024:(c + 1) * 1024] = proj(5120 + c * 1024, 6144 + c * 1024).astype(BF16)
    for c in range(2):
        gate_ref[:, c * 1024:(c + 1) * 1024] = proj(8192 + c * 1024, 9216 + c * 1024).astype(BF16)
    dt_ref[...] = jnp.dot(h, wdt_ref[...], preferred_element_type=F32)
    dtT_ref[...] = lax.dot_general(wdtT_ref[...], h, (((1,), (1,)), ((), ())),
                                   preferred_element_type=F32)


def _inproj(x2, gain, w_main, w_dt, w_dtT, qg, kg, pmat):
    t = x2.shape[0]
    tm = INP_TM
    row = lambda w: pl.BlockSpec((tm, w), lambda i: (i, 0))
    return pl.pallas_call(
        _inproj_body,
        grid=(t // tm,),
        in_specs=[
            row(D_MODEL),
            _resident((1, D_MODEL)),
            _resident(w_main.shape),
            _resident(w_dt.shape),
            _resident(w_dtT.shape),
            _resident((1, D_MODEL)),
            _resident((1, D_MODEL)),
            _resident((D_MODEL, D_MODEL)),
        ],
        out_specs=[row(1024), row(1024), row(1024), row(2048), row(3072), row(2048), row(128),
                   pl.BlockSpec((2 * SSM_HEADS, tm), lambda i: (0, i))],
        out_shape=[
            jax.ShapeDtypeStruct((t, 1024), BF16),
            jax.ShapeDtypeStruct((t, 1024), BF16),
            jax.ShapeDtypeStruct((t, 1024), BF16),
            jax.ShapeDtypeStruct((t, 2048), BF16),
            jax.ShapeDtypeStruct((t, 3072), BF16),
            jax.ShapeDtypeStruct((t, 2048), BF16),
            jax.ShapeDtypeStruct((t, 128), F32),
            jax.ShapeDtypeStruct((2 * SSM_HEADS, t), F32),
        ],
        compiler_params=pltpu.CompilerParams(
            dimension_semantics=("parallel",), vmem_limit_bytes=VMEM_LIMIT_BYTES),
        name="inproj",
    )(x2, gain, w_main, w_dt, w_dtT, qg, kg, pmat)


def _conv_body(prev_ref, cur_ref, next_ref, w_ref, b_ref, o_ref, ext_ref):
    i = pl.program_id(1)
    n = pl.num_programs(1)
    tc = cur_ref.shape[1]
    has_prev = (i > 0).astype(F32)
    has_next = (i < n - 1).astype(F32)
    ext_ref[0:CONV_HALO, :] = prev_ref[0].astype(F32) * has_prev
    ext_ref[CONV_HALO:CONV_HALO + tc, :] = cur_ref[0].astype(F32)
    ext_ref[CONV_HALO + tc:, :] = next_ref[0].astype(F32) * has_next
    cw = 512
    base = CONV_HALO - CONV_W // 2
    for c in range(CONV_DIM // cw):
        cols = slice(c * cw, (c + 1) * cw)
        acc = ext_ref[base:base + tc, cols] * w_ref[0:1, cols]
        for j in range(1, CONV_W):
            acc = acc + ext_ref[base + j:base + j + tc, cols] * w_ref[j:j + 1, cols]
        acc = acc + b_ref[:, cols]
        o_ref[0, :, cols] = (acc * _sigmoid(acc)).astype(BF16)


def _conv(xbc3, conv_w, conv_b):
    b, l, _ = xbc3.shape
    tc = CONV_TC
    hb = tc // CONV_HALO
    n_halo = l // CONV_HALO
    return pl.pallas_call(
        _conv_body,
        grid=(b, l // tc),
        in_specs=[
            pl.BlockSpec((1, CONV_HALO, CONV_DIM),
                         lambda bi, i: (bi, jnp.maximum(i * hb - 1, 0), 0)),
            pl.BlockSpec((1, tc, CONV_DIM), lambda bi, i: (bi, i, 0)),
            pl.BlockSpec((1, CONV_HALO, CONV_DIM),
                         lambda bi, i: (bi, jnp.minimum((i + 1) * hb, n_halo - 1), 0)),
            _resident((8, CONV_DIM)),
            _resident((1, CONV_DIM)),
        ],
        out_specs=pl.BlockSpec((1, tc, CONV_DIM), lambda bi, i: (bi, i, 0)),
        out_shape=jax.ShapeDtypeStruct((b, l, CONV_DIM), BF16),
        scratch_shapes=[pltpu.VMEM((tc + 2 * CONV_HALO, CONV_DIM), F32)],
        compiler_params=pltpu.CompilerParams(
            dimension_semantics=("parallel", "parallel"), vmem_limit_bytes=VMEM_LIMIT_BYTES),
        name="conv",
    )(xbc3, xbc3, xbc3, conv_w, conv_b)


def _attn_body(q_ref, k_ref, v_ref, bias_ref, o_ref, *, rows):
    j = pl.program_id(2)
    cw = ATT_HG * HEAD_DIM
    head_of_lane = lax.broadcasted_iota(jnp.int32, (GRID_W, cw), 1) // HEAD_DIM
    nk = WIN_R * GRID_W

    def row_body(i, carry):
        r = j * ATT_QR + i
        r0 = jnp.clip(r - WIN_R // 2, 0, rows - WIN_R)
        cls = r - r0
        qs = pl.multiple_of(i * GRID_W, GRID_W)
        ks = pl.multiple_of(r0 * GRID_W, GRID_W)
        q_row = q_ref[0, pl.ds(qs, GRID_W), :]
        k_win = k_ref[0, pl.ds(ks, nk), :]
        v_win = v_ref[0, pl.ds(ks, nk), :]
        q_bd = jnp.concatenate(
            [jnp.where(head_of_lane == h, q_row, jnp.zeros_like(q_row)) for h in range(ATT_HG)],
            axis=0)
        s = lax.dot_general(q_bd, k_win, (((1,), (1,)), ((), ())),
                            preferred_element_type=F32)
        s = s + bias_ref[cls].reshape(ATT_HG * GRID_W, nk)
        m = jnp.max(s, axis=-1, keepdims=True)
        p = jnp.exp(s - m)
        inv = 1.0 / jnp.sum(p, axis=-1, keepdims=True)
        o = jnp.dot(p.astype(BF16), v_win, preferred_element_type=F32) * inv
        out = o[(ATT_HG - 1) * GRID_W:]
        for h in range(ATT_HG - 2, -1, -1):
            out = jnp.where(head_of_lane == h, o[h * GRID_W:(h + 1) * GRID_W], out)
        o_ref[0, pl.ds(qs, GRID_W), :] = out.astype(BF16)
        return carry

    lax.fori_loop(0, ATT_QR, row_body, 0, unroll=True)


def _attention(q3, k3, v3, bias_tab):
    b, l, _ = q3.shape
    rows = l // GRID_W
    assert rows >= WIN_R and rows % ATT_QR == 0
    tq = ATT_QR * GRID_W
    cw = ATT_HG * HEAD_DIM
    return pl.pallas_call(
        functools.partial(_attn_body, rows=rows),
        grid=(b, N_HEADS // ATT_HG, rows // ATT_QR),
        in_specs=[
            pl.BlockSpec((1, tq, cw), lambda bi, g, j: (bi, j, g)),
            pl.BlockSpec((1, l, cw), lambda bi, g, j: (bi, 0, g)),
            pl.BlockSpec((1, l, cw), lambda bi, g, j: (bi, 0, g)),
            pl.BlockSpec((WIN_R, ATT_HG, GRID_W, WIN_R * GRID_W), lambda bi, g, j: (0, g, 0, 0)),
        ],
        out_specs=pl.BlockSpec((1, tq, cw), lambda bi, g, j: (bi, j, g)),
        out_shape=jax.ShapeDtypeStruct((b, l, N_HEADS * HEAD_DIM), BF16),
        compiler_params=pltpu.CompilerParams(
            dimension_semantics=("parallel", "parallel", "arbitrary"),
            vmem_limit_bytes=VMEM_LIMIT_BYTES),
        name="natten",
    )(q3, k3, v3, bias_tab)


def _attn_bias_table(rel_bias):
    cols = np.arange(GRID_W)
    col_start = np.clip(cols - WIN_C // 2, 0, GRID_W - WIN_C)
    col_valid = (cols[None, :] >= col_start[:, None]) & (cols[None, :] < col_start[:, None] + WIN_C)
    col_idx = np.clip(cols[None, :] - cols[:, None] + WIN_C - 1, 0, 2 * WIN_C - 2)
    onehot = (np.arange(2 * WIN_C - 1)[:, None, None] == col_idx[None]).astype(np.float32)
    full = jnp.einsum("hrc,cqk->hqrk", rel_bias.astype(F32), jnp.asarray(onehot),
                      precision=lax.Precision.HIGHEST)
    full = jnp.where(jnp.asarray(col_valid)[None, :, None, :], full, NEG_BIG)
    full = full.reshape(N_HEADS, GRID_W, (2 * WIN_R - 1) * GRID_W)
    nk = WIN_R * GRID_W
    return jnp.stack([full[:, :, (WIN_R - 1 - c) * GRID_W:(WIN_R - 1 - c) * GRID_W + nk]
                      for c in range(WIN_R)], axis=0)


def _split2(v):
    hi = v.astype(BF16)
    mid = (v - hi.astype(F32)).astype(BF16)
    return hi, mid


def _split3(v):
    hi = v.astype(BF16)
    r1 = v - hi.astype(F32)
    mid = r1.astype(BF16)
    lo = (r1 - mid.astype(F32)).astype(BF16)
    return hi, mid, lo


def _ssd_body(xf_ref, xb_ref, dtf_ref, dtb_ref, dtTf_ref, dtTb_ref, bias2_ref, alog2_ref,
              biasT_ref, alogT_ref, e2_ref, yf_ref, yb_ref, s_ref):
    i = pl.program_id(1)

    @pl.when(i == 0)
    def _():
        s_ref[...] = jnp.zeros(s_ref.shape, F32)

    q = CHUNK
    ri = lax.broadcasted_iota(jnp.int32, (q, q), 0)
    ci = lax.broadcasted_iota(jnp.int32, (q, q), 1)
    lower = ri >= ci
    upper = ri <= ci
    lower_b = lower.astype(BF16)
    upper_b = upper.astype(BF16)

    lane = lax.broadcasted_iota(jnp.int32, (1, 128), 1)
    fwd_col = (lane % 64) < SSM_HEADS
    dt = _softplus(jnp.where(fwd_col, dtf_ref[0], dtb_ref[0]) + bias2_ref[...])
    adt = dt * (-jnp.exp(alog2_ref[...]))
    pieces = jnp.concatenate(_split3(adt), axis=0)
    acum = jnp.where(
        fwd_col,
        jnp.dot(jnp.concatenate([lower_b] * 3, axis=1), pieces, preferred_element_type=F32),
        jnp.dot(jnp.concatenate([upper_b] * 3, axis=1), pieces, preferred_element_type=F32))
    last = jnp.where(fwd_col, acum[q - 1:q, :], acum[0:1, :])
    fac = jnp.where(lane < 64, jnp.exp(acum), dt * jnp.exp(last - acum))
    fac_pieces = jnp.concatenate(_split2(fac), axis=1)

    def expand(col0):
        return jnp.dot(fac_pieces, e2_ref[:, col0:col0 + GROUP_W], preferred_element_type=F32)

    rowi = lax.broadcasted_iota(jnp.int32, (2 * SSM_HEADS, 1), 0)
    fwd_row = rowi < SSM_HEADS
    dt_t = _softplus(jnp.where(fwd_row, dtTf_ref[...], dtTb_ref[...]) + biasT_ref[...])
    adt_t = dt_t * (-jnp.exp(alogT_ref[...]))
    pieces_t = jnp.concatenate(_split3(adt_t), axis=1)
    acum_t = jnp.where(
        fwd_row,
        jnp.dot(pieces_t, jnp.concatenate([upper_b] * 3, axis=0), preferred_element_type=F32),
        jnp.dot(pieces_t, jnp.concatenate([lower_b] * 3, axis=0), preferred_element_type=F32))
    g_t = acum_t - jnp.log(dt_t)

    bd_r = lax.broadcasted_iota(jnp.int32, (4 * q, 4 * SSM_HEAD_DIM), 0) // q
    bd_c = lax.broadcasted_iota(jnp.int32, (4 * q, 4 * SSM_HEAD_DIM), 1) // SSM_HEAD_DIM
    bd_mask = bd_r == bd_c

    for d, (x_ref, y_ref, causal) in enumerate(((xf_ref, yf_ref, lower), (xb_ref, yb_ref, upper))):
        for g in range(N_GROUPS):
            xs_g = x_ref[0, :, g * GROUP_W:(g + 1) * GROUP_W]
            b_g = x_ref[0, :, D_INNER + g * D_STATE:D_INNER + (g + 1) * D_STATE]
            c_g = x_ref[0, :, D_INNER + (N_GROUPS + g) * D_STATE:
                        D_INNER + (N_GROUPS + g + 1) * D_STATE]
            cb = lax.dot_general(c_g, b_g, (((1,), (1,)), ((), ())),
                                 preferred_element_type=F32)
            halves = []
            for half in range(2):
                ms = []
                for hh in range(4):
                    col = d * SSM_HEADS + g * 8 + half * 4 + hh
                    diff = acum[:, col:col + 1] - g_t[col:col + 1, :]
                    ms.append((cb * jnp.exp(jnp.where(causal, diff, NEG_BIG))).astype(BF16))
                lhs = jnp.concatenate(ms, axis=1)
                x4 = xs_g[:, half * 256:(half + 1) * 256]
                bd = jnp.where(bd_mask, jnp.concatenate([x4] * 4, axis=0), jnp.zeros((), BF16))
                halves.append(jnp.dot(lhs, bd, preferred_element_type=F32))
            y_diag = jnp.concatenate(halves, axis=1)
            s_prev = s_ref[d, g]
            ea_g = expand(d * D_INNER + g * GROUP_W)
            y_off = jnp.dot(c_g, s_prev.astype(BF16), preferred_element_type=F32) * ea_g
            y_ref[0, :, g * GROUP_W:(g + 1) * GROUP_W] = (y_diag + y_off).astype(BF16)

            w_g = expand(2 * D_INNER + d * D_INNER + g * GROUP_W)
            xw = (xs_g.astype(F32) * w_g).astype(BF16)
            b_t = jnp.transpose(b_g.astype(F32)).astype(BF16)
            decay = ea_g[q - 1:q, :] if d == 0 else ea_g[0:1, :]
            s_ref[d, g] = s_prev * decay + jnp.dot(b_t, xw, preferred_element_type=F32)


def _ssd(xsbc3, dt3, dtT, bias2, alog2, biasT, alogT, e2):
    b, l, _ = xsbc3.shape
    nc = l // CHUNK
    fwd = lambda bi, i: (bi, i, 0)
    bwd = lambda bi, i: (bi, nc - 1 - i, 0)
    return pl.pallas_call(
        _ssd_body,
        grid=(b, nc),
        in_specs=[
            pl.BlockSpec((1, CHUNK, CONV_DIM), fwd),
            pl.BlockSpec((1, CHUNK, CONV_DIM), bwd),
            pl.BlockSpec((1, CHUNK, 128), fwd),
            pl.BlockSpec((1, CHUNK, 128), bwd),
            pl.BlockSpec((2 * SSM_HEADS, CHUNK), lambda bi, i: (0, bi * nc + i)),
            pl.BlockSpec((2 * SSM_HEADS, CHUNK), lambda bi, i: (0, bi * nc + nc - 1 - i)),
            _resident((1, 128)),
            _resident((1, 128)),
            _resident((2 * SSM_HEADS, 1)),
            _resident((2 * SSM_HEADS, 1)),
            _resident(e2.shape),
        ],
        out_specs=[pl.BlockSpec((1, CHUNK, D_INNER), fwd),
                   pl.BlockSpec((1, CHUNK, D_INNER), bwd)],
        out_shape=[jax.ShapeDtypeStruct((b, l, D_INNER), BF16),
                   jax.ShapeDtypeStruct((b, l, D_INNER), BF16)],
        scratch_shapes=[pltpu.VMEM((2, N_GROUPS, D_STATE, GROUP_W), F32)],
        compiler_params=pltpu.CompilerParams(
            dimension_semantics=("parallel", "arbitrary"), vmem_limit_bytes=VMEM_LIMIT_BYTES),
        name="ssd",
    )(xsbc3, xsbc3, dt3, dt3, dtT, dtT, bias2, alog2, biasT, alogT, e2)


def _out_body(x_ref, attn_ref, yf_ref, yb_ref, xs_ref, z_ref, gate_ref, dsk_ref, nrm_ref,
              wap_ref, wsp_ref, wo_ref, o_ref):
    z = z_ref[...].astype(F32)
    y = (yf_ref[...].astype(F32) + yb_ref[...].astype(F32)
         + dsk_ref[...] * xs_ref[...].astype(F32)) * (z * _sigmoid(z))
    parts = []
    for g in range(N_GROUPS):
        cols = slice(g * GROUP_W, (g + 1) * GROUP_W)
        parts.append(_rms_normed(y[:, cols], nrm_ref[:, cols]).astype(BF16))
    ssm = jnp.concatenate(parts, axis=1)
    g_attn = gate_ref[:, :D_MODEL].astype(F32)
    g_ssm = gate_ref[:, D_MODEL:].astype(F32)
    merged = (_sigmoid(g_attn) * jnp.dot(attn_ref[...], wap_ref[...], preferred_element_type=F32)
              + _sigmoid(g_ssm) * jnp.dot(ssm, wsp_ref[...], preferred_element_type=F32))
    o_ref[...] = x_ref[...] + jnp.dot(merged.astype(BF16), wo_ref[...],
                                      preferred_element_type=F32)


def _merge_out(x2, attn2, yf2, yb2, xsbc2, z2, gate2, dsk, nrm, wap, wsp, wo):
    t = x2.shape[0]
    tm = OUT_TM
    row = lambda w: pl.BlockSpec((tm, w), lambda i: (i, 0))
    return pl.pallas_call(
        _out_body,
        grid=(t // tm,),
        in_specs=[
            row(D_MODEL), row(1024), row(D_INNER), row(D_INNER), row(D_INNER), row(D_INNER),
            row(2 * D_MODEL),
            _resident((1, D_INNER)), _resident((1, D_INNER)),
            _resident(wap.shape), _resident(wsp.shape), _resident(wo.shape),
        ],
        out_specs=row(D_MODEL),
        out_shape=jax.ShapeDtypeStruct((t, D_MODEL), F32),
        compiler_params=pltpu.CompilerParams(
            dimension_semantics=("parallel",), vmem_limit_bytes=VMEM_LIMIT_BYTES),
        name="merge_out",
    )(x2, attn2, yf2, yb2, xsbc2, z2, gate2, dsk, nrm, wap, wsp, wo)


def _ffn_weights(w_up, w_down):
    nck = D_FF // FFN_CK
    a = w_up[:, :D_FF].reshape(D_MODEL, nck, FFN_CK)
    b = w_up[:, D_FF:].reshape(D_MODEL, nck, FFN_CK)
    wu_r = jnp.concatenate([a, b], axis=-1).reshape(D_MODEL, 2 * D_FF).astype(BF16)
    return wu_r, w_down.astype(BF16)


def _expansion_matrix():
    r = jnp.arange(128)[:, None]
    c = jnp.arange(128 * SSM_HEAD_DIM)[None, :] // SSM_HEAD_DIM
    e = (r == c).astype(BF16)
    return jnp.concatenate([e, e], axis=0)


def _layer_weights(layer, ln_ffn1, w_ffn1_up, w_ffn1_down, ln_mix, w_in, q_norm, k_norm, rel_bias,
                   conv_w, conv_b, dt_bias_fwd, dt_bias_bwd, a_log_fwd, a_log_bwd, d_skip,
                   ssm_norm, w_attn_proj, w_ssm_proj, w_out, ln_ffn2, w_ffn2_up, w_ffn2_down):
    w = {}
    w["ln1"] = ln_ffn1[layer].reshape(1, D_MODEL)
    w["ffn1"] = _ffn_weights(w_ffn1_up[layer], w_ffn1_down[layer])
    w["ln2"] = ln_ffn2[layer].reshape(1, D_MODEL)
    w["ffn2"] = _ffn_weights(w_ffn2_up[layer], w_ffn2_down[layer])
    w["ln_mix"] = ln_mix[layer].reshape(1, D_MODEL)
    wi = w_in[layer]
    dt0 = 3 * D_MODEL + D_INNER + CONV_DIM
    w["w_main"] = jnp.concatenate([wi[:, :dt0], wi[:, dt0 + 2 * SSM_HEADS:]], axis=1).astype(BF16)
    wdt = wi[:, dt0:dt0 + 2 * SSM_HEADS]
    w["w_dt"] = jnp.concatenate([wdt, wdt], axis=1).astype(BF16)
    w["w_dtT"] = jnp.transpose(wdt).astype(BF16)
    w["qg"] = (jnp.tile(q_norm[layer], N_HEADS) * (HEAD_DIM ** -0.5)).reshape(1, D_MODEL)
    w["kg"] = jnp.tile(k_norm[layer], N_HEADS).reshape(1, D_MODEL)
    w["bias_tab"] = _attn_bias_table(rel_bias[layer])
    w["conv_w"] = jnp.concatenate(
        [conv_w[layer], jnp.zeros((8 - CONV_W, CONV_DIM), F32)], axis=0)
    w["conv_b"] = conv_b[layer].reshape(1, CONV_DIM)
    bias = jnp.concatenate([dt_bias_fwd[layer], dt_bias_bwd[layer]])
    alog = jnp.concatenate([a_log_fwd[layer], a_log_bwd[layer]])
    w["bias2"] = jnp.tile(bias, 2).reshape(1, 128)
    w["alog2"] = jnp.tile(alog, 2).reshape(1, 128)
    w["biasT"] = bias.reshape(2 * SSM_HEADS, 1)
    w["alogT"] = alog.reshape(2 * SSM_HEADS, 1)
    w["dsk"] = jnp.repeat(d_skip[layer], SSM_HEAD_DIM).reshape(1, D_INNER)
    w["nrm"] = ssm_norm[layer].reshape(1, D_INNER)
    w["wap"] = w_attn_proj[layer].astype(BF16)
    w["wsp"] = w_ssm_proj[layer].astype(BF16)
    w["wo"] = w_out[layer].astype(BF16)
    return w


def _head_mean_matrix():
    r = jnp.arange(D_MODEL)[:, None] // HEAD_DIM
    c = jnp.arange(D_MODEL)[None, :] // HEAD_DIM
    return jnp.where(r == c, 1.0 / HEAD_DIM, 0.0).astype(BF16)


def _encoder_layer(x2, b, l, w, pmat, e2):
    t = b * l
    x2 = _ffn(x2, w["ln1"], *w["ffn1"])
    q, k, v, z, xbc, gate, dt, dt_t = _inproj(
        x2, w["ln_mix"], w["w_main"], w["w_dt"], w["w_dtT"], w["qg"], w["kg"], pmat)
    xsbc = _conv(xbc.reshape(b, l, CONV_DIM), w["conv_w"], w["conv_b"])
    attn = _attention(q.reshape(b, l, -1), k.reshape(b, l, -1), v.reshape(b, l, -1),
                      w["bias_tab"])
    yf, yb = _ssd(xsbc, dt.reshape(b, l, 128), dt_t, w["bias2"], w["alog2"], w["biasT"],
                  w["alogT"], e2)
    x2 = _merge_out(x2, attn.reshape(t, -1), yf.reshape(t, D_INNER), yb.reshape(t, D_INNER),
                    xsbc.reshape(t, CONV_DIM), z, gate, w["dsk"], w["nrm"],
                    w["wap"], w["wsp"], w["wo"])
    return _ffn(x2, w["ln2"], *w["ffn2"])


def kernel(x_prompt, x_sample, ln_ffn1, w_ffn1_up, w_ffn1_down, ln_mix, w_in, q_norm, k_norm,
           rel_bias, conv_w, conv_b, dt_bias_fwd, dt_bias_bwd, a_log_fwd, a_log_bwd, d_skip,
           ssm_norm, w_attn_proj, w_ssm_proj, w_out, ln_ffn2, w_ffn2_up, w_ffn2_down):
    params = (ln_ffn1, w_ffn1_up, w_ffn1_down, ln_mix, w_in, q_norm, k_norm, rel_bias,
              conv_w, conv_b, dt_bias_fwd, dt_bias_bwd, a_log_fwd, a_log_bwd, d_skip,
              ssm_norm, w_attn_proj, w_ssm_proj, w_out, ln_ffn2, w_ffn2_up, w_ffn2_down)
    depth = ln_ffn1.shape[0]
    layers = [_layer_weights(i, *params) for i in range(depth)]
    pmat = _head_mean_matrix()
    e2 = _expansion_matrix()

    def run_trunk(x):
        b, l, d = x.shape
        x2 = x.reshape(b * l, d)
        for w in layers:
            x2 = _encoder_layer(x2, b, l, w, pmat, e2)
        return x2.reshape(b, l, d)

    return (run_trunk(x_prompt), run_trunk(x_sample))
```

```python
import functools

import jax
import jax.numpy as jnp
import numpy as np
from jax import lax
from jax.experimental import pallas as pl
from jax.experimental.pallas import tpu as pltpu

F32 = jnp.float32
BF16 = jnp.bfloat16

D_MODEL = 1024
GRID_W = 64
WIN_R = 8
WIN_C = 16
N_HEADS = 16
HEAD_DIM = 64
D_INNER = 2048
SSM_HEADS = 32
SSM_HEAD_DIM = 64
N_GROUPS = 4
D_STATE = 128
CONV_W = 5
CONV_DIM = D_INNER + 2 * N_GROUPS * D_STATE
CHUNK = 128
D_FF = 2816
RMS_EPS = 1e-6
NEG_BIG = -1e30
LOG2_E = 1.4426950408889634

VMEM_LIMIT_BYTES = 56 * 1024 * 1024

FFN_TM = 512
FFN_CK = 256
INP_TM = 256
CONV_TC = 256
CONV_HALO = 16
ATT_QR = 8
ATT_HG = 4
OUT_TM = 256
GROUP_W = D_INNER // N_GROUPS


def _resident(shape):
    nd = len(shape)
    return pl.BlockSpec(shape, lambda *_: (0,) * nd, pipeline_mode=pl.Buffered(1))


def _sigmoid(x):
    return 0.5 * jnp.tanh(0.5 * x) + 0.5


def _softplus(x):
    return jnp.maximum(x, 0.0) + jnp.log(1.0 + jnp.exp(-jnp.abs(x)))


def _rms_normed(x, gain):
    ms = jnp.mean(x * x, axis=-1, keepdims=True)
    return x * lax.rsqrt(ms + RMS_EPS) * gain


def _ffn_residual(x, g_ref, wu_ref, wd_ref):
    xn = _rms_normed(x, g_ref[...]).astype(BF16)
    acc = jnp.zeros(x.shape, F32)
    for c in range(D_FF // FFN_CK):
        ab = jnp.dot(xn, wu_ref[:, c * 2 * FFN_CK:(c + 1) * 2 * FFN_CK],
                     preferred_element_type=F32)
        a = ab[:, :FFN_CK]
        b = ab[:, FFN_CK:]
        h = (a * _sigmoid(a) * b).astype(BF16)
        acc = acc + jnp.dot(h, wd_ref[c * FFN_CK:(c + 1) * FFN_CK, :],
                            preferred_element_type=F32)
    return x + 0.5 * acc


def _ffn_body(x_ref, g_ref, wu_ref, wd_ref, o_ref):
    o_ref[...] = _ffn_residual(x_ref[...], g_ref, wu_ref, wd_ref)


def _ffn(x2, gain, wu_r, wd):
    t = x2.shape[0]
    return pl.pallas_call(
        _ffn_body,
        grid=(t // FFN_TM,),
        in_specs=[
            pl.BlockSpec((FFN_TM, D_MODEL), lambda i: (i, 0)),
            _resident((1, D_MODEL)),
            _resident((D_MODEL, 2 * D_FF)),
            _resident((D_FF, D_MODEL)),
        ],
        out_specs=pl.BlockSpec((FFN_TM, D_MODEL), lambda i: (i, 0)),
        out_shape=jax.ShapeDtypeStruct((t, D_MODEL), F32),
        compiler_params=pltpu.CompilerParams(
            dimension_semantics=("parallel",), vmem_limit_bytes=VMEM_LIMIT_BYTES),
        name="ffn",
    )(x2, gain, wu_r, wd)


def _inproj_body(x_ref, g_ref, w_ref, wdt_ref, wdtT_ref, qg_ref, kg_ref, p_ref, cw_ref, cb_ref,
                 q_ref, k_ref, v_ref, z_ref, gate_ref, dt_ref, dtT_ref, xsbc_ref,
                 cur_ref, ext_ref, *, n_tiles, tiles_per_seq):
    i = pl.program_id(0)
    tm = INP_TM
    halo = CONV_HALO

    @pl.when(i == 0)
    def _():
        ext_ref[...] = jnp.zeros(ext_ref.shape, F32)

    base = halo - CONV_W // 2

    def conv_rows(r0, nrows):
        cw = 512
        for c in range(CONV_DIM // cw):
            cols = slice(c * cw, (c + 1) * cw)
            acc = ext_ref[base + r0:base + r0 + nrows, cols] * cw_ref[0:1, cols]
            for j in range(1, CONV_W):
                acc = acc + (ext_ref[base + r0 + j:base + r0 + j + nrows, cols]
                             * cw_ref[j:j + 1, cols])
            acc = acc + cb_ref[:, cols]
            xsbc_ref[r0:r0 + nrows, cols] = (acc * _sigmoid(acc)).astype(BF16)

    conv_rows(0, tm - halo)

    h = _rms_normed(x_ref[...], g_ref[...]).astype(BF16)

    def proj(lo, hi):
        return jnp.dot(h, w_ref[:, lo:hi], preferred_element_type=F32)

    def head_normed(r, gain):
        ms = jnp.dot((r * r).astype(BF16), p_ref[...], preferred_element_type=F32)
        return (r * lax.rsqrt(ms + RMS_EPS) * gain).astype(BF16)

    for c in range(3):
        cur_ref[:, c * 1024:(c + 1) * 1024] = proj(5120 + c * 1024, 6144 + c * 1024)
    has_next = jnp.logical_and(i < n_tiles, (i - 1) % tiles_per_seq != tiles_per_seq - 1)
    ext_ref[halo + tm:, :] = jnp.where(has_next, cur_ref[0:halo, :], 0.0)
    conv_rows(tm - halo, halo)

    q_ref[...] = head_normed(proj(0, 1024), qg_ref[...])
    k_ref[...] = head_normed(proj(1024, 2048), kg_ref[...])
    v_ref[...] = proj(2048, 3072).astype(BF16)
    for c in range(2):
        z_ref[:, c * 1024:(c + 1) * 1024] = proj(3072 + c * 1024, 4096 + c * 1024).astype(BF16)
    for c in range(2):
        gate_ref[:, c * 1024:(c + 1) * 1024] = proj(8192 + c * 1024,
                                                    9216 + c * 1024).astype(BF16)
    dt_ref[...] = jnp.dot(h, wdt_ref[...], preferred_element_type=F32)
    dtT_ref[...] = lax.dot_general(wdtT_ref[...], h, (((1,), (1,)), ((), ())),
                                   preferred_element_type=F32)

    has_prev = i % tiles_per_seq != 0
    tail = ext_ref[tm:tm + halo, :]
    ext_ref[halo:halo + tm, :] = cur_ref[...]
    ext_ref[0:halo, :] = jnp.where(has_prev, tail, 0.0)


def _inproj(x2, seq_len, gain, w_main, w_dt, w_dtT, qg, kg, pmat, conv_w, conv_b):
    t = x2.shape[0]
    tm = INP_TM
    nt = t // tm
    cur = lambda w: pl.BlockSpec((tm, w), lambda i: (jnp.minimum(i, nt - 1), 0))
    return pl.pallas_call(
        functools.partial(_inproj_body, n_tiles=nt, tiles_per_seq=seq_len // tm),
        grid=(nt + 1,),
        in_specs=[
            cur(D_MODEL),
            _resident((1, D_MODEL)),
            _resident(w_main.shape),
            _resident(w_dt.shape),
            _resident(w_dtT.shape),
            _resident((1, D_MODEL)),
            _resident((1, D_MODEL)),
            _resident((D_MODEL, D_MODEL)),
            _resident((8, CONV_DIM)),
            _resident((1, CONV_DIM)),
        ],
        out_specs=[cur(1024), cur(1024), cur(1024), cur(2048), cur(2048), cur(128),
                   pl.BlockSpec((2 * SSM_HEADS, tm), lambda i: (0, jnp.minimum(i, nt - 1))),
                   pl.BlockSpec((tm, CONV_DIM), lambda i: (jnp.maximum(i - 1, 0), 0))],
        out_shape=[
            jax.ShapeDtypeStruct((t, 1024), BF16),
            jax.ShapeDtypeStruct((t, 1024), BF16),
            jax.ShapeDtypeStruct((t, 1024), BF16),
            jax.ShapeDtypeStruct((t, 2048), BF16),
            jax.ShapeDtypeStruct((t, 2048), BF16),
            jax.ShapeDtypeStruct((t, 128), F32),
            jax.ShapeDtypeStruct((2 * SSM_HEADS, t), F32),
            jax.ShapeDtypeStruct((t, CONV_DIM), BF16),
        ],
        scratch_shapes=[pltpu.VMEM((tm, CONV_DIM), F32),
                        pltpu.VMEM((tm + 2 * CONV_HALO, CONV_DIM), F32)],
        compiler_params=pltpu.CompilerParams(
            dimension_semantics=("arbitrary",), vmem_limit_bytes=VMEM_LIMIT_BYTES),
        name="inproj",
    )(x2, gain, w_main, w_dt, w_dtT, qg, kg, pmat, conv_w, conv_b)


def _attn_body(q_ref, k_ref, v_ref, bias_ref, o_ref, *, rows):
    j = pl.program_id(2)
    cw = ATT_HG * HEAD_DIM
    head_of_lane = lax.broadcasted_iota(jnp.int32, (GRID_W, cw), 1) // HEAD_DIM
    nk = WIN_R * GRID_W

    def row_body(i, carry):
        r = j * ATT_QR + i
        r0 = jnp.clip(r - WIN_R // 2, 0, rows - WIN_R)
        cls = r - r0
        qs = pl.multiple_of(i * GRID_W, GRID_W)
        ks = pl.multiple_of(r0 * GRID_W, GRID_W)
        q_row = q_ref[0, pl.ds(qs, GRID_W), :]
        k_win = k_ref[0, pl.ds(ks, nk), :]
        v_win = v_ref[0, pl.ds(ks, nk), :]
        q_bd = jnp.concatenate(
            [jnp.where(head_of_lane == h, q_row, jnp.zeros_like(q_row)) for h in range(ATT_HG)],
            axis=0)
        s = lax.dot_general(q_bd, k_win, (((1,), (1,)), ((), ())),
                            preferred_element_type=F32)
        s = s + bias_ref[cls].reshape(ATT_HG * GRID_W, nk)
        m = jnp.max(s, axis=-1, keepdims=True)
        p = jnp.exp2(s - m)
        inv = 1.0 / jnp.sum(p, axis=-1, keepdims=True)
        o = jnp.dot(p.astype(BF16), v_win, preferred_element_type=F32) * inv
        out = o[(ATT_HG - 1) * GRID_W:]
        for h in range(ATT_HG - 2, -1, -1):
            out = jnp.where(head_of_lane == h, o[h * GRID_W:(h + 1) * GRID_W], out)
        o_ref[0, pl.ds(qs, GRID_W), :] = out.astype(BF16)
        return carry

    lax.fori_loop(0, ATT_QR, row_body, 0, unroll=True)


def _attention(q3, k3, v3, bias_tab):
    b, l, _ = q3.shape
    rows = l // GRID_W
    assert rows >= WIN_R and rows % ATT_QR == 0
    tq = ATT_QR * GRID_W
    cw = ATT_HG * HEAD_DIM
    return pl.pallas_call(
        functools.partial(_attn_body, rows=rows),
        grid=(b, N_HEADS // ATT_HG, rows // ATT_QR),
        in_specs=[
            pl.BlockSpec((1, tq, cw), lambda bi, g, j: (bi, j, g)),
            pl.BlockSpec((1, l, cw), lambda bi, g, j: (bi, 0, g)),
            pl.BlockSpec((1, l, cw), lambda bi, g, j: (bi, 0, g)),
            pl.BlockSpec((WIN_R, ATT_HG, GRID_W, WIN_R * GRID_W), lambda bi, g, j: (0, g, 0, 0)),
        ],
        out_specs=pl.BlockSpec((1, tq, cw), lambda bi, g, j: (bi, j, g)),
        out_shape=jax.ShapeDtypeStruct((b, l, N_HEADS * HEAD_DIM), BF16),
        compiler_params=pltpu.CompilerParams(
            dimension_semantics=("parallel", "parallel", "arbitrary"),
            vmem_limit_bytes=VMEM_LIMIT_BYTES),
        name="natten",
    )(q3, k3, v3, bias_tab)


def _attn_bias_table(rel_bias):
    cols = np.arange(GRID_W)
    col_start = np.clip(cols - WIN_C // 2, 0, GRID_W - WIN_C)
    col_valid = (cols[None, :] >= col_start[:, None]) & (cols[None, :] < col_start[:, None] + WIN_C)
    col_idx = np.clip(cols[None, :] - cols[:, None] + WIN_C - 1, 0, 2 * WIN_C - 2)
    onehot = (np.arange(2 * WIN_C - 1)[:, None, None] == col_idx[None]).astype(np.float32)
    full = jnp.einsum("hrc,cqk->hqrk", rel_bias.astype(F32), jnp.asarray(onehot),
                      precision=lax.Precision.HIGHEST)
    full = jnp.where(jnp.asarray(col_valid)[None, :, None, :], full * LOG2_E, NEG_BIG)
    full = full.reshape(N_HEADS, GRID_W, (2 * WIN_R - 1) * GRID_W)
    nk = WIN_R * GRID_W
    return jnp.stack([full[:, :, (WIN_R - 1 - c) * GRID_W:(WIN_R - 1 - c) * GRID_W + nk]
                      for c in range(WIN_R)], axis=0)


def _split3(v):
    hi = v.astype(BF16)
    r1 = v - hi.astype(F32)
    mid = r1.astype(BF16)
    lo = (r1 - mid.astype(F32)).astype(BF16)
    return hi, mid, lo


def _tri_masks():
    q = CHUNK
    ri = lax.broadcasted_iota(jnp.int32, (q, q), 0)
    ci = lax.broadcasted_iota(jnp.int32, (q, q), 1)
    return ri, ci


def _row_major_scalars(dtT_ref, biasT_ref, alogT_ref):
    q = CHUNK
    ri, ci = _tri_masks()
    lower_b = (ri >= ci).astype(BF16)
    upper_b = (ri <= ci).astype(BF16)
    fwd_row = lax.broadcasted_iota(jnp.int32, (2 * SSM_HEADS, 1), 0) < SSM_HEADS
    dt_t = _softplus(dtT_ref[...] + biasT_ref[...])
    adt_t = dt_t * (-jnp.exp(alogT_ref[...]))
    pieces_t = jnp.concatenate(_split3(adt_t), axis=1)
    acum_t = jnp.where(
        fwd_row,
        jnp.dot(pieces_t, jnp.concatenate([upper_b] * 3, axis=0), preferred_element_type=F32),
        jnp.dot(pieces_t, jnp.concatenate([lower_b] * 3, axis=0), preferred_element_type=F32))
    last_t = jnp.where(fwd_row, acum_t[:, q - 1:q], acum_t[:, 0:1])
    g_t = acum_t - jnp.log(dt_t)
    w_t = dt_t * jnp.exp(last_t - acum_t)
    return dt_t, acum_t, g_t, w_t


def _block_diag4(x4):
    q = CHUNK
    shape = (4 * q, 4 * SSM_HEAD_DIM)
    keep = (lax.broadcasted_iota(jnp.int32, shape, 0) // q
            == lax.broadcasted_iota(jnp.int32, shape, 1) // SSM_HEAD_DIM)
    return jnp.where(keep, jnp.concatenate([x4] * 4, axis=0), jnp.zeros((), x4.dtype))


def _pair_lanes(a, b):
    low = lax.broadcasted_iota(jnp.int32, a.shape, 1) < SSM_HEAD_DIM
    return jnp.where(low, a, b)


def _ssd_bstate_body(xs_ref, b_ref, dt_ref, dtT_ref, bias2_ref, alog2_ref, biasT_ref, alogT_ref,
                     sbin_ref, s_ref):
    @pl.when(pl.program_id(1) == 0)
    def _():
        s_ref[...] = jnp.zeros(s_ref.shape, F32)

    q = CHUNK
    _, _, _, w_t = _row_major_scalars(dtT_ref, biasT_ref, alogT_ref)
    adt = _softplus(dt_ref[0] + bias2_ref[...]) * (-jnp.exp(alog2_ref[...]))
    chunk_decay = jnp.broadcast_to(jnp.exp(jnp.sum(adt, axis=0, keepdims=True)), (8, 128))

    for g in range(N_GROUPS):
        b_t = jnp.transpose(b_ref[0, :, g * D_STATE:(g + 1) * D_STATE].astype(F32))
        for half in range(2):
            heads = [SSM_HEADS + g * 8 + half * 4 + hh for hh in range(4)]
            lhs = jnp.concatenate([(b_t * w_t[h:h + 1, :]).astype(BF16) for h in heads], axis=1)
            cols = slice(g * GROUP_W + half * 256, g * GROUP_W + (half + 1) * 256)
            new = jnp.dot(lhs, _block_diag4(xs_ref[0, :, cols]), preferred_element_type=F32)
            dec = [jnp.broadcast_to(chunk_decay[:, h:h + 1], (8, 128)) for h in heads]
            decay = jnp.concatenate([_pair_lanes(dec[0], dec[1]), _pair_lanes(dec[2], dec[3])],
                                    axis=1)[0:1, :]
            scols = slice(half * 256, (half + 1) * 256)
            s_prev = s_ref[g, :, scols]
            sbin_ref[0, 0, g, :, scols] = s_prev.astype(BF16)
            s_ref[g, :, scols] = s_prev * decay + new


def _ssd_bstate(xsbc3, dt3, dtT, bias2, alog2, biasT, alogT):
    b, l, _ = xsbc3.shape
    nc = l // CHUNK
    rev = lambda bi, i: (bi, nc - 1 - i, 0)
    return pl.pallas_call(
        _ssd_bstate_body,
        grid=(b, nc),
        in_specs=[
            pl.BlockSpec((1, CHUNK, D_INNER), rev),
            pl.BlockSpec((1, CHUNK, N_GROUPS * D_STATE),
                         lambda bi, i: (bi, nc - 1 - i, D_INNER // (N_GROUPS * D_STATE))),
            pl.BlockSpec((1, CHUNK, 128), rev),
            pl.BlockSpec((2 * SSM_HEADS, CHUNK), lambda bi, i: (0, bi * nc + nc - 1 - i)),
            _resident((1, 128)),
            _resident((1, 128)),
            _resident((2 * SSM_HEADS, 1)),
            _resident((2 * SSM_HEADS, 1)),
        ],
        out_specs=pl.BlockSpec((1, 1, N_GROUPS, D_STATE, GROUP_W),
                               lambda bi, i: (bi, nc - 1 - i, 0, 0, 0)),
        out_shape=jax.ShapeDtypeStruct((b, nc, N_GROUPS, D_STATE, GROUP_W), BF16),
        scratch_shapes=[pltpu.VMEM((N_GROUPS, D_STATE, GROUP_W), F32)],
        compiler_params=pltpu.CompilerParams(
            dimension_semantics=("parallel", "arbitrary"), vmem_limit_bytes=VMEM_LIMIT_BYTES),
        name="ssd_bstate",
    )(xsbc3, xsbc3, dt3, dtT, bias2, alog2, biasT, alogT)


def _ssd_main_body(x_ref, dt_ref, dtT_ref, sbin_ref, bias2_ref, alog2_ref, biasT_ref, alogT_ref,
                   y_ref, s_ref):
    @pl.when(pl.program_id(1) == 0)
    def _():
        s_ref[...] = jnp.zeros(s_ref.shape, F32)

    q = CHUNK
    ri, ci = _tri_masks()
    below = ri > ci
    on_diag = ri == ci
    lower_b = (ri >= ci).astype(BF16)
    upper_b = (ri <= ci).astype(BF16)

    lane = lax.broadcasted_iota(jnp.int32, (1, 128), 1)
    fwd_col = (lane % 64) < SSM_HEADS
    adt = _softplus(dt_ref[0] + bias2_ref[...]) * (-jnp.exp(alog2_ref[...]))
    pieces = jnp.concatenate(_split3(adt), axis=0)
    acum = jnp.where(
        fwd_col,
        jnp.dot(jnp.concatenate([lower_b] * 3, axis=1), pieces, preferred_element_type=F32),
        jnp.dot(jnp.concatenate([upper_b] * 3, axis=1), pieces, preferred_element_type=F32))

    dt_t, _, g_t, w_t = _row_major_scalars(dtT_ref, biasT_ref, alogT_ref)
    dt_sum_t = dt_t[0:SSM_HEADS, :] + dt_t[SSM_HEADS:, :]
    acum = acum * LOG2_E
    g_t = g_t * LOG2_E

    for g in range(N_GROUPS):
        b_g = x_ref[0, :, D_INNER + g * D_STATE:D_INNER + (g + 1) * D_STATE]
        c_g = x_ref[0, :, D_INNER + (N_GROUPS + g) * D_STATE:
                    D_INNER + (N_GROUPS + g + 1) * D_STATE]
        cb = lax.dot_general(c_g, b_g, (((1,), (1,)), ((), ())),
                             preferred_element_type=F32)
        b_t = jnp.transpose(b_g.astype(F32))
        for half in range(2):
            ms, bws, a_fs, a_bs = [], [], [], []
            for hh in range(4):
                h = g * 8 + half * 4 + hh
                a_f = jnp.broadcast_to(acum[:, h:h + 1], (q, q))
                a_b = jnp.broadcast_to(acum[:, SSM_HEADS + h:SSM_HEADS + h + 1], (q, q))
                arg = jnp.where(below, a_f - g_t[h:h + 1, :],
                                a_b - g_t[SSM_HEADS + h:SSM_HEADS + h + 1, :])
                decay = jnp.where(on_diag, dt_sum_t[h:h + 1, :], jnp.exp2(arg))
                ms.append((cb * decay).astype(BF16))
                bws.append((b_t * w_t[h:h + 1, :]).astype(BF16))
                a_fs.append(a_f)
                a_bs.append(a_b)
            lhs = jnp.concatenate([jnp.concatenate(ms, axis=1), jnp.concatenate(bws, axis=1)],
                                  axis=0)
            cols = slice(g * GROUP_W + half * 256, g * GROUP_W + (half + 1) * 256)
            r = jnp.dot(lhs, _block_diag4(x_ref[0, :, cols]), preferred_element_type=F32)
            scale_f = jnp.exp2(jnp.concatenate([_pair_lanes(a_fs[0], a_fs[1]),
                                                _pair_lanes(a_fs[2], a_fs[3])], axis=1))
            scale_b = jnp.exp2(jnp.concatenate([_pair_lanes(a_bs[0], a_bs[1]),
                                                _pair_lanes(a_bs[2], a_bs[3])], axis=1))
            scols = slice(half * 256, (half + 1) * 256)
            s_prev = s_ref[g, :, scols]
            y_off_f = jnp.dot(c_g, s_prev.astype(BF16), preferred_element_type=F32) * scale_f
            y_off_b = jnp.dot(c_g, sbin_ref[0, 0, g, :, scols],
                              preferred_element_type=F32) * scale_b
            y_ref[0, :, cols] = (r[0:q] + y_off_f + y_off_b).astype(BF16)
            s_ref[g, :, scols] = s_prev * scale_f[q - 1:q, :] + r[q:]


def _ssd_main(xsbc3, dt3, dtT, sbin, bias2, alog2, biasT, alogT):
    b, l, _ = xsbc3.shape
    nc = l // CHUNK
    fwd = lambda bi, i: (bi, i, 0)
    return pl.pallas_call(
        _ssd_main_body,
        grid=(b, nc),
        in_specs=[
            pl.BlockSpec((1, CHUNK, CONV_DIM), fwd),
            pl.BlockSpec((1, CHUNK, 128), fwd),
            pl.BlockSpec((2 * SSM_HEADS, CHUNK), lambda bi, i: (0, bi * nc + i)),
            pl.BlockSpec((1, 1, N_GROUPS, D_STATE, GROUP_W), lambda bi, i: (bi, i, 0, 0, 0)),
            _resident((1, 128)),
            _resident((1, 128)),
            _resident((2 * SSM_HEADS, 1)),
            _resident((2 * SSM_HEADS, 1)),
        ],
        out_specs=pl.BlockSpec((1, CHUNK, D_INNER), fwd),
        out_shape=jax.ShapeDtypeStruct((b, l, D_INNER), BF16),
        scratch_shapes=[pltpu.VMEM((N_GROUPS, D_STATE, GROUP_W), F32)],
        compiler_params=pltpu.CompilerParams(
            dimension_semantics=("parallel", "arbitrary"), vmem_limit_bytes=VMEM_LIMIT_BYTES),
        name="ssd_main",
    )(xsbc3, dt3, dtT, sbin, bias2, alog2, biasT, alogT)


def _out_body(x_ref, attn_ref, y_ref, xs_ref, z_ref, gate_ref, dsk_ref, nrm_ref,
              wap_ref, wsp_ref, wo_ref, o_ref):
    z = z_ref[...].astype(F32)
    y = (y_ref[...].astype(F32) + dsk_ref[...] * xs_ref[...].astype(F32)) * (z * _sigmoid(z))
    parts = []
    for g in range(N_GROUPS):
        cols = slice(g * GROUP_W, (g + 1) * GROUP_W)
        parts.append(_rms_normed(y[:, cols], nrm_ref[:, cols]).astype(BF16))
    ssm = jnp.concatenate(parts, axis=1)
    g_attn = gate_ref[:, :D_MODEL].astype(F32)
    g_ssm = gate_ref[:, D_MODEL:].astype(F32)
    merged = (_sigmoid(g_attn) * jnp.dot(attn_ref[...], wap_ref[...], preferred_element_type=F32)
              + _sigmoid(g_ssm) * jnp.dot(ssm, wsp_ref[...], preferred_element_type=F32))
    o_ref[...] = x_ref[...] + jnp.dot(merged.astype(BF16), wo_ref[...],
                                      preferred_element_type=F32)


def _merge_out(x2, attn2, y2, xsbc2, z2, gate2, dsk, nrm, wap, wsp, wo):
    t = x2.shape[0]
    tm = OUT_TM
    row = lambda w: pl.BlockSpec((tm, w), lambda i: (i, 0))
    return pl.pallas_call(
        _out_body,
        grid=(t // tm,),
        in_specs=[
            row(D_MODEL), row(1024), row(D_INNER), row(D_INNER), row(D_INNER),
            row(2 * D_MODEL),
            _resident((1, D_INNER)), _resident((1, D_INNER)),
            _resident(wap.shape), _resident(wsp.shape), _resident(wo.shape),
        ],
        out_specs=row(D_MODEL),
        out_shape=jax.ShapeDtypeStruct((t, D_MODEL), F32),
        compiler_params=pltpu.CompilerParams(
            dimension_semantics=("parallel",), vmem_limit_bytes=VMEM_LIMIT_BYTES),
        name="merge_out",
    )(x2, attn2, y2, xsbc2, z2, gate2, dsk, nrm, wap, wsp, wo)


def _ffn_weights(w_up, w_down):
    nck = D_FF // FFN_CK
    a = w_up[:, :D_FF].reshape(D_MODEL, nck, FFN_CK)
    b = w_up[:, D_FF:].reshape(D_MODEL, nck, FFN_CK)
    wu_r = jnp.concatenate([a, b], axis=-1).reshape(D_MODEL, 2 * D_FF).astype(BF16)
    return wu_r, w_down.astype(BF16)


def _layer_weights(layer, ln_ffn1, w_ffn1_up, w_ffn1_down, ln_mix, w_in, q_norm, k_norm, rel_bias,
                   conv_w, conv_b, dt_bias_fwd, dt_bias_bwd, a_log_fwd, a_log_bwd, d_skip,
                   ssm_norm, w_attn_proj, w_ssm_proj, w_out, ln_ffn2, w_ffn2_up, w_ffn2_down):
    w = {}
    w["ln1"] = ln_ffn1[layer].reshape(1, D_MODEL)
    w["ffn1"] = _ffn_weights(w_ffn1_up[layer], w_ffn1_down[layer])
    w["ln2"] = ln_ffn2[layer].reshape(1, D_MODEL)
    w["ffn2"] = _ffn_weights(w_ffn2_up[layer], w_ffn2_down[layer])
    w["ln_mix"] = ln_mix[layer].reshape(1, D_MODEL)
    wi = w_in[layer]
    dt0 = 3 * D_MODEL + D_INNER + CONV_DIM
    w["w_main"] = jnp.concatenate([wi[:, :dt0], wi[:, dt0 + 2 * SSM_HEADS:]], axis=1).astype(BF16)
    wdt = wi[:, dt0:dt0 + 2 * SSM_HEADS]
    w["w_dt"] = jnp.concatenate([wdt, wdt], axis=1).astype(BF16)
    w["w_dtT"] = jnp.transpose(wdt).astype(BF16)
    w["qg"] = (jnp.tile(q_norm[layer], N_HEADS)
               * (HEAD_DIM ** -0.5 * LOG2_E)).reshape(1, D_MODEL)
    w["kg"] = jnp.tile(k_norm[layer], N_HEADS).reshape(1, D_MODEL)
    w["bias_tab"] = _attn_bias_table(rel_bias[layer])
    w["conv_w"] = jnp.concatenate(
        [conv_w[layer], jnp.zeros((8 - CONV_W, CONV_DIM), F32)], axis=0)
    w["conv_b"] = conv_b[layer].reshape(1, CONV_DIM)
    bias = jnp.concatenate([dt_bias_fwd[layer], dt_bias_bwd[layer]])
    alog = jnp.concatenate([a_log_fwd[layer], a_log_bwd[layer]])
    w["bias2"] = jnp.tile(bias, 2).reshape(1, 128)
    w["alog2"] = jnp.tile(alog, 2).reshape(1, 128)
    w["biasT"] = bias.reshape(2 * SSM_HEADS, 1)
    w["alogT"] = alog.reshape(2 * SSM_HEADS, 1)
    w["dsk"] = jnp.repeat(d_skip[layer], SSM_HEAD_DIM).reshape(1, D_INNER)
    w["nrm"] = ssm_norm[layer].reshape(1, D_INNER)
    w["wap"] = w_attn_proj[layer].astype(BF16)
    w["wsp"] = w_ssm_proj[layer].astype(BF16)
    w["wo"] = w_out[layer].astype(BF16)
    return w


def _head_mean_matrix():
    r = jnp.arange(D_MODEL)[:, None] // HEAD_DIM
    c = jnp.arange(D_MODEL)[None, :] // HEAD_DIM
    return jnp.where(r == c, 1.0 / HEAD_DIM, 0.0).astype(BF16)


def _encoder_layer(x2, b, l, w, pmat):
    t = b * l
    x2 = _ffn(x2, w["ln1"], *w["ffn1"])
    q, k, v, z, gate, dt, dt_t, xsbc = _inproj(
        x2, l, w["ln_mix"], w["w_main"], w["w_dt"], w["w_dtT"], w["qg"], w["kg"], pmat,
        w["conv_w"], w["conv_b"])
    attn = _attention(q.reshape(b, l, -1), k.reshape(b, l, -1), v.reshape(b, l, -1),
                      w["bias_tab"])
    xsbc3 = xsbc.reshape(b, l, CONV_DIM)
    dt3 = dt.reshape(b, l, 128)
    scalars = (w["bias2"], w["alog2"], w["biasT"], w["alogT"])
    s_bwd_in = _ssd_bstate(xsbc3, dt3, dt_t, *scalars)
    y = _ssd_main(xsbc3, dt3, dt_t, s_bwd_in, *scalars)
    x2 = _merge_out(x2, attn.reshape(t, -1), y.reshape(t, D_INNER), xsbc, z, gate,
                    w["dsk"], w["nrm"], w["wap"], w["wsp"], w["wo"])
    return _ffn(x2, w["ln2"], *w["ffn2"])


def kernel(x_prompt, x_sample, ln_ffn1, w_ffn1_up, w_ffn1_down, ln_mix, w_in, q_norm, k_norm,
           rel_bias, conv_w, conv_b, dt_bias_fwd, dt_bias_bwd, a_log_fwd, a_log_bwd, d_skip,
           ssm_norm, w_attn_proj, w_ssm_proj, w_out, ln_ffn2, w_ffn2_up, w_ffn2_down):
    params = (ln_ffn1, w_ffn1_up, w_ffn1_down, ln_mix, w_in, q_norm, k_norm, rel_bias,
              conv_w, conv_b, dt_bias_fwd, dt_bias_bwd, a_log_fwd, a_log_bwd, d_skip,
              ssm_norm, w_attn_proj, w_ssm_proj, w_out, ln_ffn2, w_ffn2_up, w_ffn2_down)
    depth = ln_ffn1.shape[0]
    layers = [_layer_weights(i, *params) for i in range(depth)]
    pmat = _head_mean_matrix()

    def run_trunk(x):
        b, l, d = x.shape
        x2 = x.reshape(b * l, d)
        for w in layers:
            x2 = _encoder_layer(x2, b, l, w, pmat)
        return x2.reshape(b, l, d)

    return (run_trunk(x_prompt), run_trunk(x_sample))
```

```python
import functools

import jax
import jax.numpy as jnp
import numpy as np
from jax import lax
from jax.experimental import pallas as pl
from jax.experimental.pallas import tpu as pltpu

F32 = jnp.float32
BF16 = jnp.bfloat16

D_MODEL = 1024
GRID_W = 64
WIN_R = 8
WIN_C = 16
N_HEADS = 16
HEAD_DIM = 64
D_INNER = 2048
SSM_HEADS = 32
SSM_HEAD_DIM = 64
N_GROUPS = 4
D_STATE = 128
CONV_W = 5
CONV_DIM = D_INNER + 2 * N_GROUPS * D_STATE
CHUNK = 128
D_FF = 2816
RMS_EPS = 1e-6
NEG_BIG = -1e30
LOG2_E = 1.4426950408889634

VMEM_LIMIT_BYTES = 56 * 1024 * 1024

FFN_TM = 1024
FFN_CK = 256
INP_TM = 256
CONV_HALO = 16
SSD_CPS = 2
ATT_QR = 16
ATT_HG = 4
OUT_TM = 512
GROUP_W = D_INNER // N_GROUPS


def _resident(shape):
    nd = len(shape)
    return pl.BlockSpec(shape, lambda *_: (0,) * nd, pipeline_mode=pl.Buffered(1))


def _sigmoid(x):
    return 0.5 * jnp.tanh(0.5 * x) + 0.5


def _softplus(x):
    return jnp.maximum(x, 0.0) + jnp.log(1.0 + jnp.exp(-jnp.abs(x)))


def _rms_normed(x, gain):
    ms = jnp.mean(x * x, axis=-1, keepdims=True)
    return x * lax.rsqrt(ms + RMS_EPS) * gain


def _ffn_residual(x, g_ref, wu_ref, wd_ref):
    xn = _rms_normed(x, g_ref[...]).astype(BF16)
    acc = jnp.zeros(x.shape, F32)
    for c in range(D_FF // FFN_CK):
        ab = jnp.dot(xn, wu_ref[:, c * 2 * FFN_CK:(c + 1) * 2 * FFN_CK],
                     preferred_element_type=F32)
        a = ab[:, :FFN_CK]
        b = ab[:, FFN_CK:]
        h = (a * _sigmoid(a) * b).astype(BF16)
        acc = acc + jnp.dot(h, wd_ref[c * FFN_CK:(c + 1) * FFN_CK, :],
                            preferred_element_type=F32)
    return x + 0.5 * acc


def _ffn_body(x_ref, g_ref, wu_ref, wd_ref, o_ref):
    o_ref[...] = _ffn_residual(x_ref[...], g_ref, wu_ref, wd_ref)


def _ffn(x2, gain, wu_r, wd):
    t = x2.shape[0]
    return pl.pallas_call(
        _ffn_body,
        grid=(t // FFN_TM,),
        in_specs=[
            pl.BlockSpec((FFN_TM, D_MODEL), lambda i: (i, 0)),
            _resident((1, D_MODEL)),
            _resident((D_MODEL, 2 * D_FF)),
            _resident((D_FF, D_MODEL)),
        ],
        out_specs=pl.BlockSpec((FFN_TM, D_MODEL), lambda i: (i, 0)),
        out_shape=jax.ShapeDtypeStruct((t, D_MODEL), F32),
        compiler_params=pltpu.CompilerParams(
            dimension_semantics=("parallel",), vmem_limit_bytes=VMEM_LIMIT_BYTES),
        name="ffn",
    )(x2, gain, wu_r, wd)


def _inproj_body(x_ref, g_ref, w_ref, wdt_ref, wdtT_ref, qg_ref, kg_ref, p_ref, cw_ref, cb_ref,
                 q_ref, k_ref, v_ref, z_ref, gate_ref, dt_ref, dtT_ref, xsbc_ref,
                 cur_ref, ext_ref, *, n_tiles, tiles_per_seq):
    i = pl.program_id(0)
    tm = INP_TM
    halo = CONV_HALO

    @pl.when(i == 0)
    def _():
        ext_ref[...] = jnp.zeros(ext_ref.shape, F32)

    base = halo - CONV_W // 2

    def conv_rows(r0, nrows):
        cw = 512
        for c in range(CONV_DIM // cw):
            cols = slice(c * cw, (c + 1) * cw)
            acc = ext_ref[base + r0:base + r0 + nrows, cols] * cw_ref[0:1, cols]
            for j in range(1, CONV_W):
                acc = acc + (ext_ref[base + r0 + j:base + r0 + j + nrows, cols]
                             * cw_ref[j:j + 1, cols])
            acc = acc + cb_ref[:, cols]
            xsbc_ref[r0:r0 + nrows, cols] = (acc * _sigmoid(acc)).astype(BF16)

    conv_rows(0, tm - halo)

    h = _rms_normed(x_ref[...], g_ref[...]).astype(BF16)

    def proj(lo, hi):
        return jnp.dot(h, w_ref[:, lo:hi], preferred_element_type=F32)

    def head_normed(r, gain):
        ms = jnp.dot((r * r).astype(BF16), p_ref[...], preferred_element_type=F32)
        return (r * lax.rsqrt(ms + RMS_EPS) * gain).astype(BF16)

    for c in range(3):
        cur_ref[:, c * 1024:(c + 1) * 1024] = proj(5120 + c * 1024, 6144 + c * 1024)
    has_next = jnp.logical_and(i < n_tiles, (i - 1) % tiles_per_seq != tiles_per_seq - 1)
    ext_ref[halo + tm:, :] = jnp.where(has_next, cur_ref[0:halo, :], 0.0)
    conv_rows(tm - halo, halo)

    q_ref[...] = head_normed(proj(0, 1024), qg_ref[...])
    k_ref[...] = head_normed(proj(1024, 2048), kg_ref[...])
    v_ref[...] = proj(2048, 3072).astype(BF16)
    for c in range(2):
        z_ref[:, c * 1024:(c + 1) * 1024] = proj(3072 + c * 1024, 4096 + c * 1024).astype(BF16)
    for c in range(2):
        gate_ref[:, c * 1024:(c + 1) * 1024] = proj(8192 + c * 1024,
                                                    9216 + c * 1024).astype(BF16)
    dt_ref[...] = jnp.dot(h, wdt_ref[...], preferred_element_type=F32)
    dtT_ref[...] = lax.dot_general(wdtT_ref[...], h, (((1,), (1,)), ((), ())),
                                   preferred_element_type=F32)

    has_prev = i % tiles_per_seq != 0
    tail = ext_ref[tm:tm + halo, :]
    ext_ref[halo:halo + tm, :] = cur_ref[...]
    ext_ref[0:halo, :] = jnp.where(has_prev, tail, 0.0)


def _inproj(x2, seq_len, gain, w_main, w_dt, w_dtT, qg, kg, pmat, conv_w, conv_b):
    t = x2.shape[0]
    tm = INP_TM
    nt = t // tm
    cur = lambda w: pl.BlockSpec((tm, w), lambda i: (jnp.minimum(i, nt - 1), 0))
    return pl.pallas_call(
        functools.partial(_inproj_body, n_tiles=nt, tiles_per_seq=seq_len // tm),
        grid=(nt + 1,),
        in_specs=[
            cur(D_MODEL),
            _resident((1, D_MODEL)),
            _resident(w_main.shape),
            _resident(w_dt.shape),
            _resident(w_dtT.shape),
            _resident((1, D_MODEL)),
            _resident((1, D_MODEL)),
            _resident((D_MODEL, D_MODEL)),
            _resident((8, CONV_DIM)),
            _resident((1, CONV_DIM)),
        ],
        out_specs=[cur(1024), cur(1024), cur(1024), cur(2048), cur(2048), cur(128),
                   pl.BlockSpec((2 * SSM_HEADS, tm), lambda i: (0, jnp.minimum(i, nt - 1))),
                   pl.BlockSpec((tm, CONV_DIM), lambda i: (jnp.maximum(i - 1, 0), 0))],
        out_shape=[
            jax.ShapeDtypeStruct((t, 1024), BF16),
            jax.ShapeDtypeStruct((t, 1024), BF16),
            jax.ShapeDtypeStruct((t, 1024), BF16),
            jax.ShapeDtypeStruct((t, 2048), BF16),
            jax.ShapeDtypeStruct((t, 2048), BF16),
            jax.ShapeDtypeStruct((t, 128), F32),
            jax.ShapeDtypeStruct((2 * SSM_HEADS, t), F32),
            jax.ShapeDtypeStruct((t, CONV_DIM), BF16),
        ],
        scratch_shapes=[pltpu.VMEM((tm, CONV_DIM), F32),
                        pltpu.VMEM((tm + 2 * CONV_HALO, CONV_DIM), F32)],
        compiler_params=pltpu.CompilerParams(
            dimension_semantics=("arbitrary",), vmem_limit_bytes=VMEM_LIMIT_BYTES),
        name="inproj",
    )(x2, gain, w_main, w_dt, w_dtT, qg, kg, pmat, conv_w, conv_b)


def _attn_body(q_ref, k_ref, v_ref, bias_ref, o_ref, *, rows):
    j = pl.program_id(2)
    cw = ATT_HG * HEAD_DIM
    head_of_lane = lax.broadcasted_iota(jnp.int32, (GRID_W, cw), 1) // HEAD_DIM
    nk = WIN_R * GRID_W

    def row_body(i, carry):
        r = j * ATT_QR + i
        r0 = jnp.clip(r - WIN_R // 2, 0, rows - WIN_R)
        cls = r - r0
        qs = pl.multiple_of(i * GRID_W, GRID_W)
        ks = pl.multiple_of(r0 * GRID_W, GRID_W)
        q_row = q_ref[0, pl.ds(qs, GRID_W), :]
        k_win = k_ref[0, pl.ds(ks, nk), :]
        v_win = v_ref[0, pl.ds(ks, nk), :]
        q_bd = jnp.concatenate(
            [jnp.where(head_of_lane == h, q_row, jnp.zeros_like(q_row)) for h in range(ATT_HG)],
            axis=0)
        s = lax.dot_general(q_bd, k_win, (((1,), (1,)), ((), ())),
                            preferred_element_type=F32)
        s = s + bias_ref[cls].reshape(ATT_HG * GRID_W, nk)
        m = jnp.max(s, axis=-1, keepdims=True)
        p = jnp.exp2(s - m)
        inv = 1.0 / jnp.sum(p, axis=-1, keepdims=True)
        o = jnp.dot(p.astype(BF16), v_win, preferred_element_type=F32) * inv
        out = o[(ATT_HG - 1) * GRID_W:]
        for h in range(ATT_HG - 2, -1, -1):
            out = jnp.where(head_of_lane == h, o[h * GRID_W:(h + 1) * GRID_W], out)
        o_ref[0, pl.ds(qs, GRID_W), :] = out.astype(BF16)
        return carry

    lax.fori_loop(0, ATT_QR, row_body, 0, unroll=True)


def _attention(q3, k3, v3, bias_tab):
    b, l, _ = q3.shape
    rows = l // GRID_W
    assert rows >= WIN_R and rows % ATT_QR == 0
    tq = ATT_QR * GRID_W
    cw = ATT_HG * HEAD_DIM
    return pl.pallas_call(
        functools.partial(_attn_body, rows=rows),
        grid=(b, N_HEADS // ATT_HG, rows // ATT_QR),
        in_specs=[
            pl.BlockSpec((1, tq, cw), lambda bi, g, j: (bi, j, g)),
            pl.BlockSpec((1, l, cw), lambda bi, g, j: (bi, 0, g)),
            pl.BlockSpec((1, l, cw), lambda bi, g, j: (bi, 0, g)),
            pl.BlockSpec((WIN_R, ATT_HG, GRID_W, WIN_R * GRID_W), lambda bi, g, j: (0, g, 0, 0)),
        ],
        out_specs=pl.BlockSpec((1, tq, cw), lambda bi, g, j: (bi, j, g)),
        out_shape=jax.ShapeDtypeStruct((b, l, N_HEADS * HEAD_DIM), BF16),
        compiler_params=pltpu.CompilerParams(
            dimension_semantics=("parallel", "parallel", "arbitrary"),
            vmem_limit_bytes=VMEM_LIMIT_BYTES),
        name="natten",
    )(q3, k3, v3, bias_tab)


def _attn_bias_table(rel_bias):
    cols = np.arange(GRID_W)
    col_start = np.clip(cols - WIN_C // 2, 0, GRID_W - WIN_C)
    col_valid = (cols[None, :] >= col_start[:, None]) & (cols[None, :] < col_start[:, None] + WIN_C)
    col_idx = np.clip(cols[None, :] - cols[:, None] + WIN_C - 1, 0, 2 * WIN_C - 2)
    onehot = (np.arange(2 * WIN_C - 1)[:, None, None] == col_idx[None]).astype(np.float32)
    full = jnp.einsum("hrc,cqk->hqrk", rel_bias.astype(F32), jnp.asarray(onehot),
                      precision=lax.Precision.HIGHEST)
    full = jnp.where(jnp.asarray(col_valid)[None, :, None, :], full * LOG2_E, NEG_BIG)
    full = full.reshape(N_HEADS, GRID_W, (2 * WIN_R - 1) * GRID_W)
    nk = WIN_R * GRID_W
    return jnp.stack([full[:, :, (WIN_R - 1 - c) * GRID_W:(WIN_R - 1 - c) * GRID_W + nk]
                      for c in range(WIN_R)], axis=0)


def _split3(v):
    hi = v.astype(BF16)
    r1 = v - hi.astype(F32)
    mid = r1.astype(BF16)
    lo = (r1 - mid.astype(F32)).astype(BF16)
    return hi, mid, lo


def _tri_masks():
    q = CHUNK
    ri = lax.broadcasted_iota(jnp.int32, (q, q), 0)
    ci = lax.broadcasted_iota(jnp.int32, (q, q), 1)
    return ri, ci


def _row_major_scalars(dt_raw_t, biasT_ref, alogT_ref):
    q = CHUNK
    ri, ci = _tri_masks()
    lower_b = (ri >= ci).astype(BF16)
    upper_b = (ri <= ci).astype(BF16)
    fwd_row = lax.broadcasted_iota(jnp.int32, (2 * SSM_HEADS, 1), 0) < SSM_HEADS
    dt_t = _softplus(dt_raw_t + biasT_ref[...])
    adt_t = dt_t * (-jnp.exp(alogT_ref[...]))
    pieces_t = jnp.concatenate(_split3(adt_t), axis=1)
    acum_t = jnp.where(
        fwd_row,
        jnp.dot(pieces_t, jnp.concatenate([upper_b] * 3, axis=0), preferred_element_type=F32),
        jnp.dot(pieces_t, jnp.concatenate([lower_b] * 3, axis=0), preferred_element_type=F32))
    last_t = jnp.where(fwd_row, acum_t[:, q - 1:q], acum_t[:, 0:1])
    g_t = acum_t - jnp.log(dt_t)
    w_t = dt_t * jnp.exp(last_t - acum_t)
    return dt_t, acum_t, g_t, w_t


def _block_diag4(x4):
    q = CHUNK
    shape = (4 * q, 4 * SSM_HEAD_DIM)
    keep = (lax.broadcasted_iota(jnp.int32, shape, 0) // q
            == lax.broadcasted_iota(jnp.int32, shape, 1) // SSM_HEAD_DIM)
    return jnp.where(keep, jnp.concatenate([x4] * 4, axis=0), jnp.zeros((), x4.dtype))


def _pair_lanes(a, b):
    low = lax.broadcasted_iota(jnp.int32, a.shape, 1) < SSM_HEAD_DIM
    return jnp.where(low, a, b)


def _ssd_bstate_body(xs_ref, b_ref, dt_ref, dtT_ref, bias2_ref, alog2_ref, biasT_ref, alogT_ref,
                     sbin_ref, s_ref):
    @pl.when(pl.program_id(1) == 0)
    def _():
        s_ref[...] = jnp.zeros(s_ref.shape, F32)

    q = CHUNK
    for cc in range(SSD_CPS - 1, -1, -1):
        rows = slice(cc * q, (cc + 1) * q)
        _, _, _, w_t = _row_major_scalars(dtT_ref[:, rows], biasT_ref, alogT_ref)
        adt = _softplus(dt_ref[0, rows, :] + bias2_ref[...]) * (-jnp.exp(alog2_ref[...]))
        chunk_decay = jnp.broadcast_to(jnp.exp(jnp.sum(adt, axis=0, keepdims=True)), (8, 128))

        for g in range(N_GROUPS):
            b_t = jnp.transpose(
                b_ref[0, rows, g * D_STATE:(g + 1) * D_STATE].astype(F32))
            for half in range(2):
                heads = [SSM_HEADS + g * 8 + half * 4 + hh for hh in range(4)]
                lhs = jnp.concatenate([(b_t * w_t[h:h + 1, :]).astype(BF16) for h in heads],
                                      axis=1)
                cols = slice(g * GROUP_W + half * 256, g * GROUP_W + (half + 1) * 256)
                new = jnp.dot(lhs, _block_diag4(xs_ref[0, rows, cols]),
                              preferred_element_type=F32)
                dec = [jnp.broadcast_to(chunk_decay[:, h:h + 1], (8, 128)) for h in heads]
                decay = jnp.concatenate(
                    [_pair_lanes(dec[0], dec[1]), _pair_lanes(dec[2], dec[3])],
                    axis=1)[0:1, :]
                scols = slice(half * 256, (half + 1) * 256)
                s_prev = s_ref[g, :, scols]
                sbin_ref[0, cc, g, :, scols] = s_prev.astype(BF16)
                s_ref[g, :, scols] = s_prev * decay + new


def _ssd_bstate(xsbc3, dt3, dtT, bias2, alog2, biasT, alogT):
    b, l, _ = xsbc3.shape
    nc = l // CHUNK
    ns = nc // SSD_CPS
    tq = SSD_CPS * CHUNK
    rev = lambda bi, i: (bi, ns - 1 - i, 0)
    return pl.pallas_call(
        _ssd_bstate_body,
        grid=(b, ns),
        in_specs=[
            pl.BlockSpec((1, tq, D_INNER), rev),
            pl.BlockSpec((1, tq, N_GROUPS * D_STATE),
                         lambda bi, i: (bi, ns - 1 - i, D_INNER // (N_GROUPS * D_STATE))),
            pl.BlockSpec((1, tq, 128), rev),
            pl.BlockSpec((2 * SSM_HEADS, tq), lambda bi, i: (0, bi * ns + ns - 1 - i)),
            _resident((1, 128)),
            _resident((1, 128)),
            _resident((2 * SSM_HEADS, 1)),
            _resident((2 * SSM_HEADS, 1)),
        ],
        out_specs=pl.BlockSpec((1, SSD_CPS, N_GROUPS, D_STATE, GROUP_W),
                               lambda bi, i: (bi, ns - 1 - i, 0, 0, 0)),
        out_shape=jax.ShapeDtypeStruct((b, nc, N_GROUPS, D_STATE, GROUP_W), BF16),
        scratch_shapes=[pltpu.VMEM((N_GROUPS, D_STATE, GROUP_W), F32)],
        compiler_params=pltpu.CompilerParams(
            dimension_semantics=("parallel", "arbitrary"), vmem_limit_bytes=VMEM_LIMIT_BYTES),
        name="ssd_bstate",
    )(xsbc3, xsbc3, dt3, dtT, bias2, alog2, biasT, alogT)


def _ssd_main_body(x_ref, dt_ref, dtT_ref, sbin_ref, bias2_ref, alog2_ref, biasT_ref, alogT_ref,
                   y_ref, s_ref):
    @pl.when(pl.program_id(1) == 0)
    def _():
        s_ref[...] = jnp.zeros(s_ref.shape, F32)

    q = CHUNK
    ri, ci = _tri_masks()
    below = ri > ci
    on_diag = ri == ci
    lower_b = (ri >= ci).astype(BF16)
    upper_b = (ri <= ci).astype(BF16)

    lane = lax.broadcasted_iota(jnp.int32, (1, 128), 1)
    fwd_col = (lane % 64) < SSM_HEADS

    for cc in range(SSD_CPS):
        rows = slice(cc * q, (cc + 1) * q)
        adt = _softplus(dt_ref[0, rows, :] + bias2_ref[...]) * (-jnp.exp(alog2_ref[...]))
        pieces = jnp.concatenate(_split3(adt), axis=0)
        acum = jnp.where(
            fwd_col,
            jnp.dot(jnp.concatenate([lower_b] * 3, axis=1), pieces, preferred_element_type=F32),
            jnp.dot(jnp.concatenate([upper_b] * 3, axis=1), pieces, preferred_element_type=F32))

        dt_t, _, g_t, w_t = _row_major_scalars(dtT_ref[:, rows], biasT_ref, alogT_ref)
        dt_sum_t = dt_t[0:SSM_HEADS, :] + dt_t[SSM_HEADS:, :]
        acum = acum * LOG2_E
        g_t = g_t * LOG2_E

        for g in range(N_GROUPS):
            b_g = x_ref[0, rows, D_INNER + g * D_STATE:D_INNER + (g + 1) * D_STATE]
            c_g = x_ref[0, rows, D_INNER + (N_GROUPS + g) * D_STATE:
                        D_INNER + (N_GROUPS + g + 1) * D_STATE]
            cb = lax.dot_general(c_g, b_g, (((1,), (1,)), ((), ())),
                                 preferred_element_type=F32)
            b_t = jnp.transpose(b_g.astype(F32))
            for half in range(2):
                ms, bws, a_fs, a_bs = [], [], [], []
                for hh in range(4):
                    h = g * 8 + half * 4 + hh
                    a_f = jnp.broadcast_to(acum[:, h:h + 1], (q, q))
                    a_b = jnp.broadcast_to(acum[:, SSM_HEADS + h:SSM_HEADS + h + 1], (q, q))
                    arg = jnp.where(below, a_f - g_t[h:h + 1, :],
                                    a_b - g_t[SSM_HEADS + h:SSM_HEADS + h + 1, :])
                    decay = jnp.where(on_diag, dt_sum_t[h:h + 1, :], jnp.exp2(arg))
                    ms.append((cb * decay).astype(BF16))
                    bws.append((b_t * w_t[h:h + 1, :]).astype(BF16))
                    a_fs.append(a_f)
                    a_bs.append(a_b)
                lhs = jnp.concatenate(
                    [jnp.concatenate(ms, axis=1), jnp.concatenate(bws, axis=1)], axis=0)
                cols = slice(g * GROUP_W + half * 256, g * GROUP_W + (half + 1) * 256)
                r = jnp.dot(lhs, _block_diag4(x_ref[0, rows, cols]),
                            preferred_element_type=F32)
                scale_f = jnp.exp2(jnp.concatenate([_pair_lanes(a_fs[0], a_fs[1]),
                                                    _pair_lanes(a_fs[2], a_fs[3])], axis=1))
                scale_b = jnp.exp2(jnp.concatenate([_pair_lanes(a_bs[0], a_bs[1]),
                                                    _pair_lanes(a_bs[2], a_bs[3])], axis=1))
                scols = slice(half * 256, (half + 1) * 256)
                s_prev = s_ref[g, :, scols]
                y_off_f = jnp.dot(c_g, s_prev.astype(BF16),
                                  preferred_element_type=F32) * scale_f
                y_off_b = jnp.dot(c_g, sbin_ref[0, cc, g, :, scols],
                                  preferred_element_type=F32) * scale_b
                y_ref[0, rows, cols] = (r[0:q] + y_off_f + y_off_b).astype(BF16)
                s_ref[g, :, scols] = s_prev * scale_f[q - 1:q, :] + r[q:]


def _ssd_main(xsbc3, dt3, dtT, sbin, bias2, alog2, biasT, alogT):
    b, l, _ = xsbc3.shape
    ns = l // (CHUNK * SSD_CPS)
    tq = SSD_CPS * CHUNK
    fwd = lambda bi, i: (bi, i, 0)
    return pl.pallas_call(
        _ssd_main_body,
        grid=(b, ns),
        in_specs=[
            pl.BlockSpec((1, tq, CONV_DIM), fwd),
            pl.BlockSpec((1, tq, 128), fwd),
            pl.BlockSpec((2 * SSM_HEADS, tq), lambda bi, i: (0, bi * ns + i)),
            pl.BlockSpec((1, SSD_CPS, N_GROUPS, D_STATE, GROUP_W),
                         lambda bi, i: (bi, i, 0, 0, 0)),
            _resident((1, 128)),
            _resident((1, 128)),
            _resident((2 * SSM_HEADS, 1)),
            _resident((2 * SSM_HEADS, 1)),
        ],
        out_specs=pl.BlockSpec((1, tq, D_INNER), fwd),
        out_shape=jax.ShapeDtypeStruct((b, l, D_INNER), BF16),
        scratch_shapes=[pltpu.VMEM((N_GROUPS, D_STATE, GROUP_W), F32)],
        compiler_params=pltpu.CompilerParams(
            dimension_semantics=("parallel", "arbitrary"), vmem_limit_bytes=VMEM_LIMIT_BYTES),
        name="ssd_main",
    )(xsbc3, dt3, dtT, sbin, bias2, alog2, biasT, alogT)


def _out_body(x_ref, attn_ref, y_ref, xs_ref, z_ref, gate_ref, dsk_ref, nrm_ref,
              wap_ref, wsp_ref, wo_ref, o_ref):
    z = z_ref[...].astype(F32)
    y = (y_ref[...].astype(F32) + dsk_ref[...] * xs_ref[...].astype(F32)) * (z * _sigmoid(z))
    parts = []
    for g in range(N_GROUPS):
        cols = slice(g * GROUP_W, (g + 1) * GROUP_W)
        parts.append(_rms_normed(y[:, cols], nrm_ref[:, cols]).astype(BF16))
    ssm = jnp.concatenate(parts, axis=1)
    g_attn = gate_ref[:, :D_MODEL].astype(F32)
    g_ssm = gate_ref[:, D_MODEL:].astype(F32)
    merged = (_sigmoid(g_attn) * jnp.dot(attn_ref[...], wap_ref[...], preferred_element_type=F32)
              + _sigmoid(g_ssm) * jnp.dot(ssm, wsp_ref[...], preferred_element_type=F32))
    o_ref[...] = x_ref[...] + jnp.dot(merged.astype(BF16), wo_ref[...],
                                      preferred_element_type=F32)


def _merge_out(x2, attn2, y2, xsbc2, z2, gate2, dsk, nrm, wap, wsp, wo):
    t = x2.shape[0]
    tm = OUT_TM
    row = lambda w: pl.BlockSpec((tm, w), lambda i: (i, 0))
    return pl.pallas_call(
        _out_body,
        grid=(t // tm,),
        in_specs=[
            row(D_MODEL), row(1024), row(D_INNER), row(D_INNER), row(D_INNER),
            row(2 * D_MODEL),
            _resident((1, D_INNER)), _resident((1, D_INNER)),
            _resident(wap.shape), _resident(wsp.shape), _resident(wo.shape),
        ],
        out_specs=row(D_MODEL),
        out_shape=jax.ShapeDtypeStruct((t, D_MODEL), F32),
        compiler_params=pltpu.CompilerParams(
            dimension_semantics=("parallel",), vmem_limit_bytes=VMEM_LIMIT_BYTES),
        name="merge_out",
    )(x2, attn2, y2, xsbc2, z2, gate2, dsk, nrm, wap, wsp, wo)


def _ffn_weights(w_up, w_down):
    nck = D_FF // FFN_CK
    a = w_up[:, :D_FF].reshape(D_MODEL, nck, FFN_CK)
    b = w_up[:, D_FF:].reshape(D_MODEL, nck, FFN_CK)
    wu_r = jnp.concatenate([a, b], axis=-1).reshape(D_MODEL, 2 * D_FF).astype(BF16)
    return wu_r, w_down.astype(BF16)


def _layer_weights(layer, ln_ffn1, w_ffn1_up, w_ffn1_down, ln_mix, w_in, q_norm, k_norm, rel_bias,
                   conv_w, conv_b, dt_bias_fwd, dt_bias_bwd, a_log_fwd, a_log_bwd, d_skip,
                   ssm_norm, w_attn_proj, w_ssm_proj, w_out, ln_ffn2, w_ffn2_up, w_ffn2_down):
    w = {}
    w["ln1"] = ln_ffn1[layer].reshape(1, D_MODEL)
    w["ffn1"] = _ffn_weights(w_ffn1_up[layer], w_ffn1_down[layer])
    w["ln2"] = ln_ffn2[layer].reshape(1, D_MODEL)
    w["ffn2"] = _ffn_weights(w_ffn2_up[layer], w_ffn2_down[layer])
    w["ln_mix"] = ln_mix[layer].reshape(1, D_MODEL)
    wi = w_in[layer]
    dt0 = 3 * D_MODEL + D_INNER + CONV_DIM
    w["w_main"] = jnp.concatenate([wi[:, :dt0], wi[:, dt0 + 2 * SSM_HEADS:]], axis=1).astype(BF16)
    wdt = wi[:, dt0:dt0 + 2 * SSM_HEADS]
    w["w_dt"] = jnp.concatenate([wdt, wdt], axis=1).astype(BF16)
    w["w_dtT"] = jnp.transpose(wdt).astype(BF16)
    w["qg"] = (jnp.tile(q_norm[layer], N_HEADS)
               * (HEAD_DIM ** -0.5 * LOG2_E)).reshape(1, D_MODEL)
    w["kg"] = jnp.tile(k_norm[layer], N_HEADS).reshape(1, D_MODEL)
    w["bias_tab"] = _attn_bias_table(rel_bias[layer])
    w["conv_w"] = jnp.concatenate(
        [conv_w[layer], jnp.zeros((8 - CONV_W, CONV_DIM), F32)], axis=0)
    w["conv_b"] = conv_b[layer].reshape(1, CONV_DIM)
    bias = jnp.concatenate([dt_bias_fwd[layer], dt_bias_bwd[layer]])
    alog = jnp.concatenate([a_log_fwd[layer], a_log_bwd[layer]])
    w["bias2"] = jnp.tile(bias, 2).reshape(1, 128)
    w["alog2"] = jnp.tile(alog, 2).reshape(1, 128)
    w["biasT"] = bias.reshape(2 * SSM_HEADS, 1)
    w["alogT"] = alog.reshape(2 * SSM_HEADS, 1)
    w["dsk"] = jnp.repeat(d_skip[layer], SSM_HEAD_DIM).reshape(1, D_INNER)
    w["nrm"] = ssm_norm[layer].reshape(1, D_INNER)
    w["wap"] = w_attn_proj[layer].astype(BF16)
    w["wsp"] = w_ssm_proj[layer].astype(BF16)
    w["wo"] = w_out[layer].astype(BF16)
    return w


def _head_mean_matrix():
    r = jnp.arange(D_MODEL)[:, None] // HEAD_DIM
    c = jnp.arange(D_MODEL)[None, :] // HEAD_DIM
    return jnp.where(r == c, 1.0 / HEAD_DIM, 0.0).astype(BF16)


def _encoder_layer(x2, b, l, w, pmat):
    t = b * l
    x2 = _ffn(x2, w["ln1"], *w["ffn1"])
    q, k, v, z, gate, dt, dt_t, xsbc = _inproj(
        x2, l, w["ln_mix"], w["w_main"], w["w_dt"], w["w_dtT"], w["qg"], w["kg"], pmat,
        w["conv_w"], w["conv_b"])
    attn = _attention(q.reshape(b, l, -1), k.reshape(b, l, -1), v.reshape(b, l, -1),
                      w["bias_tab"])
    xsbc3 = xsbc.reshape(b, l, CONV_DIM)
    dt3 = dt.reshape(b, l, 128)
    scalars = (w["bias2"], w["alog2"], w["biasT"], w["alogT"])
    s_bwd_in = _ssd_bstate(xsbc3, dt3, dt_t, *scalars)
    y = _ssd_main(xsbc3, dt3, dt_t, s_bwd_in, *scalars)
    x2 = _merge_out(x2, attn.reshape(t, -1), y.reshape(t, D_INNER), xsbc, z, gate,
                    w["dsk"], w["nrm"], w["wap"], w["wsp"], w["wo"])
    return _ffn(x2, w["ln2"], *w["ffn2"])


def kernel(x_prompt, x_sample, ln_ffn1, w_ffn1_up, w_ffn1_down, ln_mix, w_in, q_norm, k_norm,
           rel_bias, conv_w, conv_b, dt_bias_fwd, dt_bias_bwd, a_log_fwd, a_log_bwd, d_skip,
           ssm_norm, w_attn_proj, w_ssm_proj, w_out, ln_ffn2, w_ffn2_up, w_ffn2_down):
    params = (ln_ffn1, w_ffn1_up, w_ffn1_down, ln_mix, w_in, q_norm, k_norm, rel_bias,
              conv_w, conv_b, dt_bias_fwd, dt_bias_bwd, a_log_fwd, a_log_bwd, d_skip,
              ssm_norm, w_attn_proj, w_ssm_proj, w_out, ln_ffn2, w_ffn2_up, w_ffn2_down)
    depth = ln_ffn1.shape[0]
    layers = [_layer_weights(i, *params) for i in range(depth)]
    pmat = _head_mean_matrix()

    def run_trunk(x):
        b, l, d = x.shape
        x2 = x.reshape(b * l, d)
        for w in layers:
            x2 = _encoder_layer(x2, b, l, w, pmat)
        return x2.reshape(b, l, d)

    return (run_trunk(x_prompt), run_trunk(x_sample))
```

```python
import functools

import jax
import jax.numpy as jnp
import numpy as np
from jax import lax
from jax.experimental import pallas as pl
from jax.experimental.pallas import tpu as pltpu

F32 = jnp.float32
BF16 = jnp.bfloat16

D_MODEL = 1024
GRID_W = 64
WIN_R = 8
WIN_C = 16
N_HEADS = 16
HEAD_DIM = 64
D_INNER = 2048
SSM_HEADS = 32
SSM_HEAD_DIM = 64
N_GROUPS = 4
D_STATE = 128
CONV_W = 5
CONV_DIM = D_INNER + 2 * N_GROUPS * D_STATE
CHUNK = 128
D_FF = 2816
RMS_EPS = 1e-6
NEG_BIG = -1e30
LOG2_E = 1.4426950408889634

VMEM_LIMIT_BYTES = 56 * 1024 * 1024

FFN_TM = 1024
FFN_CK = 256
INP_TM = 256
CONV_HALO = 16
SSD_CPS = 2
ATT_QR = 32
ATT_HG = 4
OUT_TM = 512
GROUP_W = D_INNER // N_GROUPS


def _resident(shape):
    nd = len(shape)
    return pl.BlockSpec(shape, lambda *_: (0,) * nd, pipeline_mode=pl.Buffered(1))


def _sigmoid(x):
    return 0.5 * jnp.tanh(0.5 * x) + 0.5


def _softplus(x):
    return jnp.maximum(x, 0.0) + jnp.log(1.0 + jnp.exp(-jnp.abs(x)))


def _rms_normed(x, gain):
    ms = jnp.mean(x * x, axis=-1, keepdims=True)
    return x * lax.rsqrt(ms + RMS_EPS) * gain


def _ffn_residual(x, g_ref, wu_ref, wd_ref):
    xn = _rms_normed(x, g_ref[...]).astype(BF16)
    acc = jnp.zeros(x.shape, F32)
    for c in range(D_FF // FFN_CK):
        ab = jnp.dot(xn, wu_ref[:, c * 2 * FFN_CK:(c + 1) * 2 * FFN_CK],
                     preferred_element_type=F32)
        a = ab[:, :FFN_CK]
        b = ab[:, FFN_CK:]
        h = (a * _sigmoid(a) * b).astype(BF16)
        acc = acc + jnp.dot(h, wd_ref[c * FFN_CK:(c + 1) * FFN_CK, :],
                            preferred_element_type=F32)
    return x + 0.5 * acc


def _ffn_body(x_ref, g_ref, wu_ref, wd_ref, o_ref):
    o_ref[...] = _ffn_residual(x_ref[...], g_ref, wu_ref, wd_ref)


def _ffn(x2, gain, wu_r, wd):
    t = x2.shape[0]
    return pl.pallas_call(
        _ffn_body,
        grid=(t // FFN_TM,),
        in_specs=[
            pl.BlockSpec((FFN_TM, D_MODEL), lambda i: (i, 0)),
            _resident((1, D_MODEL)),
            _resident((D_MODEL, 2 * D_FF)),
            _resident((D_FF, D_MODEL)),
        ],
        out_specs=pl.BlockSpec((FFN_TM, D_MODEL), lambda i: (i, 0)),
        out_shape=jax.ShapeDtypeStruct((t, D_MODEL), F32),
        compiler_params=pltpu.CompilerParams(
            dimension_semantics=("parallel",), vmem_limit_bytes=VMEM_LIMIT_BYTES),
        name="ffn",
    )(x2, gain, wu_r, wd)


def _inproj_body(x_ref, g_ref, w_ref, wx_ref, wdt_ref, wdtT_ref, qg_ref, kg_ref, p_ref,
                 cw_ref, cb_ref,
                 q_ref, k_ref, v_ref, z_ref, gate_ref, dt_ref, dtT_ref, xsbc_ref,
                 cur_ref, ext_ref, h_ref, *, n_tiles, tiles_per_seq):
    i = pl.program_id(0)
    tm = INP_TM
    halo = CONV_HALO

    @pl.when(i == 0)
    def _():
        ext_ref[...] = jnp.zeros(ext_ref.shape, F32)

    base = halo - CONV_W // 2

    conv_cw = 512

    def conv_rows(r0, nrows, c):
        win = nrows + 2 * halo
        g0, ng = halo // 8, nrows // 8
        sub = lax.broadcasted_iota(jnp.int32, (ng, 8, conv_cw), 1)
        cols = slice(c * conv_cw, (c + 1) * conv_cw)
        e = ext_ref[r0:r0 + win, cols].reshape(win // 8, 8, conv_cw)
        acc = e[g0:g0 + ng] * cw_ref[CONV_W // 2:CONV_W // 2 + 1, cols]
        for j in range(CONV_W):
            d = j - CONV_W // 2
            if d == 0:
                continue
            rot = pltpu.roll(e, (-d) % 8, 1)
            if d > 0:
                tap = jnp.where(sub < 8 - d, rot[g0:g0 + ng], rot[g0 + 1:g0 + ng + 1])
            else:
                tap = jnp.where(sub >= -d, rot[g0:g0 + ng], rot[g0 - 1:g0 + ng - 1])
            acc = acc + tap * cw_ref[j:j + 1, cols]
        acc = acc + cb_ref[:, cols]
        y = (acc * _sigmoid(acc)).reshape(nrows, conv_cw)
        xsbc_ref[r0:r0 + nrows, cols] = y.astype(BF16)

    h_ref[...] = _rms_normed(x_ref[...], g_ref[...]).astype(BF16)

    def proj(lo, hi):
        return jnp.dot(h_ref[...], w_ref[:, lo:hi], preferred_element_type=F32)

    def head_normed(r, gain):
        ms = jnp.dot((r * r).astype(BF16), p_ref[...], preferred_element_type=F32)
        return (r * lax.rsqrt(ms + RMS_EPS) * gain).astype(BF16)

    def store_q():
        q_ref[...] = head_normed(proj(0, 1024), qg_ref[...])

    def store_k():
        k_ref[...] = head_normed(proj(1024, 2048), kg_ref[...])

    def store_v():
        v_ref[...] = proj(2048, 3072).astype(BF16)

    def store_z(c):
        z_ref[:, c * 1024:(c + 1) * 1024] = proj(3072 + c * 1024, 4096 + c * 1024).astype(BF16)

    def store_gate(c):
        gate_ref[:, c * 1024:(c + 1) * 1024] = proj(5120 + c * 1024,
                                                    6144 + c * 1024).astype(BF16)

    mxu_work = [store_q, store_k, store_v, lambda: store_z(0), lambda: store_z(1),
                lambda: store_gate(0)]
    for c in range(CONV_DIM // conv_cw):
        conv_rows(0, tm - halo, c)
        mxu_work[c]()

    for c in range(3):
        cur_ref[:, c * 1024:(c + 1) * 1024] = jnp.dot(
            h_ref[...], wx_ref[:, c * 1024:(c + 1) * 1024], preferred_element_type=F32)
    has_next = jnp.logical_and(i < n_tiles, (i - 1) % tiles_per_seq != tiles_per_seq - 1)
    ext_ref[halo + tm:, :] = jnp.where(has_next, cur_ref[0:halo, :], 0.0)
    for c in range(CONV_DIM // conv_cw):
        conv_rows(tm - halo, halo, c)
    store_gate(1)

    dt_ref[...] = jnp.dot(h_ref[...], wdt_ref[...], preferred_element_type=F32)
    dtT_ref[...] = lax.dot_general(wdtT_ref[...], h_ref[...], (((1,), (1,)), ((), ())),
                                   preferred_element_type=F32)

    has_prev = i % tiles_per_seq != 0
    tail = ext_ref[tm:tm + halo, :]
    ext_ref[halo:halo + tm, :] = cur_ref[...]
    ext_ref[0:halo, :] = jnp.where(has_prev, tail, 0.0)


def _inproj(x2, seq_len, gain, w_a, w_x, w_dt, w_dtT, qg, kg, pmat, conv_w, conv_b):
    t = x2.shape[0]
    tm = INP_TM
    nt = t // tm
    cur = lambda w: pl.BlockSpec((tm, w), lambda i: (jnp.minimum(i, nt - 1), 0))
    return pl.pallas_call(
        functools.partial(_inproj_body, n_tiles=nt, tiles_per_seq=seq_len // tm),
        grid=(nt + 1,),
        in_specs=[
            cur(D_MODEL),
            _resident((1, D_MODEL)),
            _resident(w_a.shape),
            _resident(w_x.shape),
            _resident(w_dt.shape),
            _resident(w_dtT.shape),
            _resident((1, D_MODEL)),
            _resident((1, D_MODEL)),
            _resident((D_MODEL, D_MODEL)),
            _resident((8, CONV_DIM)),
            _resident((1, CONV_DIM)),
        ],
        out_specs=[cur(1024), cur(1024), cur(1024), cur(2048), cur(2048), cur(128),
                   pl.BlockSpec((2 * SSM_HEADS, tm), lambda i: (0, jnp.minimum(i, nt - 1))),
                   pl.BlockSpec((tm, CONV_DIM), lambda i: (jnp.maximum(i - 1, 0), 0))],
        out_shape=[
            jax.ShapeDtypeStruct((t, 1024), BF16),
            jax.ShapeDtypeStruct((t, 1024), BF16),
            jax.ShapeDtypeStruct((t, 1024), BF16),
            jax.ShapeDtypeStruct((t, 2048), BF16),
            jax.ShapeDtypeStruct((t, 2048), BF16),
            jax.ShapeDtypeStruct((t, 128), F32),
            jax.ShapeDtypeStruct((2 * SSM_HEADS, t), F32),
            jax.ShapeDtypeStruct((t, CONV_DIM), BF16),
        ],
        scratch_shapes=[pltpu.VMEM((tm, CONV_DIM), F32),
                        pltpu.VMEM((tm + 2 * CONV_HALO, CONV_DIM), F32),
                        pltpu.VMEM((tm, D_MODEL), BF16)],
        compiler_params=pltpu.CompilerParams(
            dimension_semantics=("arbitrary",), vmem_limit_bytes=VMEM_LIMIT_BYTES),
        name="inproj",
    )(x2, gain, w_a, w_x, w_dt, w_dtT, qg, kg, pmat, conv_w, conv_b)


def _attn_body(q_ref, k_ref, v_ref, bias_ref, o_ref, *, rows):
    j = pl.program_id(2)
    cw = ATT_HG * HEAD_DIM
    head_of_lane = lax.broadcasted_iota(jnp.int32, (GRID_W, cw), 1) // HEAD_DIM
    nk = WIN_R * GRID_W

    def row_body(i, carry):
        r = j * ATT_QR + i
        r0 = jnp.clip(r - WIN_R // 2, 0, rows - WIN_R)
        cls = r - r0
        qs = pl.multiple_of(i * GRID_W, GRID_W)
        ks = pl.multiple_of(r0 * GRID_W, GRID_W)
        q_row = q_ref[0, pl.ds(qs, GRID_W), :]
        k_win = k_ref[0, pl.ds(ks, nk), :]
        v_win = v_ref[0, pl.ds(ks, nk), :]
        q_bd = jnp.concatenate(
            [jnp.where(head_of_lane == h, q_row, jnp.zeros_like(q_row)) for h in range(ATT_HG)],
            axis=0)
        s = lax.dot_general(q_bd, k_win, (((1,), (1,)), ((), ())),
                            preferred_element_type=F32)
        s = s + bias_ref[cls].reshape(ATT_HG * GRID_W, nk)
        m = jnp.max(s, axis=-1, keepdims=True)
        p = jnp.exp2(s - m)
        inv = 1.0 / jnp.sum(p, axis=-1, keepdims=True)
        o = jnp.dot(p.astype(BF16), v_win, preferred_element_type=F32) * inv
        out = o[(ATT_HG - 1) * GRID_W:]
        for h in range(ATT_HG - 2, -1, -1):
            out = jnp.where(head_of_lane == h, o[h * GRID_W:(h + 1) * GRID_W], out)
        o_ref[0, pl.ds(qs, GRID_W), :] = out.astype(BF16)
        return carry

    lax.fori_loop(0, ATT_QR, row_body, 0, unroll=True)


def _attention(q3, k3, v3, bias_tab):
    b, l, _ = q3.shape
    rows = l // GRID_W
    assert rows >= WIN_R and rows % ATT_QR == 0
    tq = ATT_QR * GRID_W
    cw = ATT_HG * HEAD_DIM
    return pl.pallas_call(
        functools.partial(_attn_body, rows=rows),
        grid=(b, N_HEADS // ATT_HG, rows // ATT_QR),
        in_specs=[
            pl.BlockSpec((1, tq, cw), lambda bi, g, j: (bi, j, g)),
            pl.BlockSpec((1, l, cw), lambda bi, g, j: (bi, 0, g)),
            pl.BlockSpec((1, l, cw), lambda bi, g, j: (bi, 0, g)),
            pl.BlockSpec((WIN_R, ATT_HG, GRID_W, WIN_R * GRID_W), lambda bi, g, j: (0, g, 0, 0)),
        ],
        out_specs=pl.BlockSpec((1, tq, cw), lambda bi, g, j: (bi, j, g)),
        out_shape=jax.ShapeDtypeStruct((b, l, N_HEADS * HEAD_DIM), BF16),
        compiler_params=pltpu.CompilerParams(
            dimension_semantics=("parallel", "parallel", "arbitrary"),
            vmem_limit_bytes=VMEM_LIMIT_BYTES),
        name="natten",
    )(q3, k3, v3, bias_tab)


def _attn_bias_table(rel_bias):
    cols = np.arange(GRID_W)
    col_start = np.clip(cols - WIN_C // 2, 0, GRID_W - WIN_C)
    col_valid = (cols[None, :] >= col_start[:, None]) & (cols[None, :] < col_start[:, None] + WIN_C)
    col_idx = np.clip(cols[None, :] - cols[:, None] + WIN_C - 1, 0, 2 * WIN_C - 2)
    onehot = (np.arange(2 * WIN_C - 1)[:, None, None] == col_idx[None]).astype(np.float32)
    full = jnp.einsum("hrc,cqk->hqrk", rel_bias.astype(F32), jnp.asarray(onehot),
                      precision=lax.Precision.HIGHEST)
    full = jnp.where(jnp.asarray(col_valid)[None, :, None, :], full * LOG2_E, NEG_BIG)
    full = full.reshape(N_HEADS, GRID_W, (2 * WIN_R - 1) * GRID_W)
    nk = WIN_R * GRID_W
    return jnp.stack([full[:, :, (WIN_R - 1 - c) * GRID_W:(WIN_R - 1 - c) * GRID_W + nk]
                      for c in range(WIN_R)], axis=0)


def _split3(v):
    hi = v.astype(BF16)
    r1 = v - hi.astype(F32)
    mid = r1.astype(BF16)
    lo = (r1 - mid.astype(F32)).astype(BF16)
    return hi, mid, lo


def _tri_masks():
    q = CHUNK
    ri = lax.broadcasted_iota(jnp.int32, (q, q), 0)
    ci = lax.broadcasted_iota(jnp.int32, (q, q), 1)
    return ri, ci


def _row_major_scalars(dt_raw_t, biasT_ref, alogT_ref):
    q = CHUNK
    ri, ci = _tri_masks()
    lower_b = (ri >= ci).astype(BF16)
    upper_b = (ri <= ci).astype(BF16)
    fwd_row = lax.broadcasted_iota(jnp.int32, (2 * SSM_HEADS, 1), 0) < SSM_HEADS
    dt_t = _softplus(dt_raw_t + biasT_ref[...])
    adt_t = dt_t * (-jnp.exp(alogT_ref[...]))
    pieces_t = jnp.concatenate(_split3(adt_t), axis=1)
    acum_t = jnp.where(
        fwd_row,
        jnp.dot(pieces_t, jnp.concatenate([upper_b] * 3, axis=0), preferred_element_type=F32),
        jnp.dot(pieces_t, jnp.concatenate([lower_b] * 3, axis=0), preferred_element_type=F32))
    last_t = jnp.where(fwd_row, acum_t[:, q - 1:q], acum_t[:, 0:1])
    g_t = acum_t - jnp.log(dt_t)
    w_t = dt_t * jnp.exp(last_t - acum_t)
    return dt_t, acum_t, g_t, w_t


def _block_diag4(x4):
    q = CHUNK
    shape = (4 * q, 4 * SSM_HEAD_DIM)
    keep = (lax.broadcasted_iota(jnp.int32, shape, 0) // q
            == lax.broadcasted_iota(jnp.int32, shape, 1) // SSM_HEAD_DIM)
    return jnp.where(keep, jnp.concatenate([x4] * 4, axis=0), jnp.zeros((), x4.dtype))


def _pair_lanes(a, b):
    low = lax.broadcasted_iota(jnp.int32, a.shape, 1) < SSM_HEAD_DIM
    return jnp.where(low, a, b)


def _ssd_bstate_body(xs_ref, b_ref, dt_ref, dtT_ref, bias2_ref, alog2_ref, biasT_ref, alogT_ref,
                     sbin_ref, s_ref):
    @pl.when(pl.program_id(1) == 0)
    def _():
        s_ref[...] = jnp.zeros(s_ref.shape, F32)

    q = CHUNK
    for cc in range(SSD_CPS - 1, -1, -1):
        rows = slice(cc * q, (cc + 1) * q)
        _, _, _, w_t = _row_major_scalars(dtT_ref[:, rows], biasT_ref, alogT_ref)
        w_t16 = w_t.astype(BF16)
        adt = _softplus(dt_ref[0, rows, :] + bias2_ref[...]) * (-jnp.exp(alog2_ref[...]))
        chunk_decay = jnp.broadcast_to(jnp.exp(jnp.sum(adt, axis=0, keepdims=True)), (8, 128))

        for g in range(N_GROUPS):
            b_t16 = jnp.transpose(
                b_ref[0, rows, g * D_STATE:(g + 1) * D_STATE].astype(F32)).astype(BF16)
            for half in range(2):
                heads = [SSM_HEADS + g * 8 + half * 4 + hh for hh in range(4)]
                lhs = jnp.concatenate([b_t16 * w_t16[h:h + 1, :] for h in heads], axis=1)
                cols = slice(g * GROUP_W + half * 256, g * GROUP_W + (half + 1) * 256)
                new = jnp.dot(lhs, _block_diag4(xs_ref[0, rows, cols]),
                              preferred_element_type=F32)
                dec = [jnp.broadcast_to(chunk_decay[:, h:h + 1], (8, 128)) for h in heads]
                decay = jnp.concatenate(
                    [_pair_lanes(dec[0], dec[1]), _pair_lanes(dec[2], dec[3])],
                    axis=1)[0:1, :]
                scols = slice(half * 256, (half + 1) * 256)
                s_prev = s_ref[g, :, scols]
                sbin_ref[0, cc, g, :, scols] = s_prev.astype(BF16)
                s_ref[g, :, scols] = s_prev * decay + new


def _ssd_bstate(xsbc3, dt3, dtT, bias2, alog2, biasT, alogT):
    b, l, _ = xsbc3.shape
    nc = l // CHUNK
    ns = nc // SSD_CPS
    tq = SSD_CPS * CHUNK
    rev = lambda bi, i: (bi, ns - 1 - i, 0)
    return pl.pallas_call(
        _ssd_bstate_body,
        grid=(b, ns),
        in_specs=[
            pl.BlockSpec((1, tq, D_INNER), rev),
            pl.BlockSpec((1, tq, N_GROUPS * D_STATE),
                         lambda bi, i: (bi, ns - 1 - i, D_INNER // (N_GROUPS * D_STATE))),
            pl.BlockSpec((1, tq, 128), rev),
            pl.BlockSpec((2 * SSM_HEADS, tq), lambda bi, i: (0, bi * ns + ns - 1 - i)),
            _resident((1, 128)),
            _resident((1, 128)),
            _resident((2 * SSM_HEADS, 1)),
            _resident((2 * SSM_HEADS, 1)),
        ],
        out_specs=pl.BlockSpec((1, SSD_CPS, N_GROUPS, D_STATE, GROUP_W),
                               lambda bi, i: (bi, ns - 1 - i, 0, 0, 0)),
        out_shape=jax.ShapeDtypeStruct((b, nc, N_GROUPS, D_STATE, GROUP_W), BF16),
        scratch_shapes=[pltpu.VMEM((N_GROUPS, D_STATE, GROUP_W), F32)],
        compiler_params=pltpu.CompilerParams(
            dimension_semantics=("parallel", "arbitrary"), vmem_limit_bytes=VMEM_LIMIT_BYTES),
        name="ssd_bstate",
    )(xsbc3, xsbc3, dt3, dtT, bias2, alog2, biasT, alogT)


def _ssd_main_body(x_ref, dt_ref, dtT_ref, sbin_ref, bias2_ref, alog2_ref, biasT_ref, alogT_ref,
                   y_ref, s_ref):
    @pl.when(pl.program_id(1) == 0)
    def _():
        s_ref[...] = jnp.zeros(s_ref.shape, F32)

    q = CHUNK
    ri, ci = _tri_masks()
    below = ri > ci
    on_diag = ri == ci
    lower_b = (ri >= ci).astype(BF16)
    upper_b = (ri <= ci).astype(BF16)

    lane = lax.broadcasted_iota(jnp.int32, (1, 128), 1)
    fwd_col = (lane % 64) < SSM_HEADS

    for cc in range(SSD_CPS):
        rows = slice(cc * q, (cc + 1) * q)
        adt = _softplus(dt_ref[0, rows, :] + bias2_ref[...]) * (-jnp.exp(alog2_ref[...]))
        pieces = jnp.concatenate(_split3(adt), axis=0)
        acum = jnp.where(
            fwd_col,
            jnp.dot(jnp.concatenate([lower_b] * 3, axis=1), pieces, preferred_element_type=F32),
            jnp.dot(jnp.concatenate([upper_b] * 3, axis=1), pieces, preferred_element_type=F32))

        dt_t, _, g_t, w_t = _row_major_scalars(dtT_ref[:, rows], biasT_ref, alogT_ref)
        dt_sum_t = dt_t[0:SSM_HEADS, :] + dt_t[SSM_HEADS:, :]
        acum = acum * LOG2_E
        g_t = g_t * LOG2_E
        w_t16 = w_t.astype(BF16)

        for g in range(N_GROUPS):
            b_g = x_ref[0, rows, D_INNER + g * D_STATE:D_INNER + (g + 1) * D_STATE]
            c_g = x_ref[0, rows, D_INNER + (N_GROUPS + g) * D_STATE:
                        D_INNER + (N_GROUPS + g + 1) * D_STATE]
            cb = lax.dot_general(c_g, b_g, (((1,), (1,)), ((), ())),
                                 preferred_element_type=F32)
            cb16 = cb.astype(BF16)
            b_t16 = jnp.transpose(b_g.astype(F32)).astype(BF16)
            for half in range(2):
                ms, bws, a_fs, a_bs = [], [], [], []
                for hh in range(4):
                    h = g * 8 + half * 4 + hh
                    a_f = jnp.broadcast_to(acum[:, h:h + 1], (q, q))
                    a_b = jnp.broadcast_to(acum[:, SSM_HEADS + h:SSM_HEADS + h + 1], (q, q))
                    arg = jnp.where(below, a_f - g_t[h:h + 1, :],
                                    a_b - g_t[SSM_HEADS + h:SSM_HEADS + h + 1, :])
                    decay = jnp.where(on_diag, dt_sum_t[h:h + 1, :], jnp.exp2(arg))
                    ms.append(cb16 * decay.astype(BF16))
                    bws.append(b_t16 * w_t16[h:h + 1, :])
                    a_fs.append(a_f)
                    a_bs.append(a_b)
                lhs = jnp.concatenate(
                    [jnp.concatenate(ms, axis=1), jnp.concatenate(bws, axis=1)], axis=0)
                cols = slice(g * GROUP_W + half * 256, g * GROUP_W + (half + 1) * 256)
                r = jnp.dot(lhs, _block_diag4(x_ref[0, rows, cols]),
                            preferred_element_type=F32)
                scale_f = jnp.exp2(jnp.concatenate([_pair_lanes(a_fs[0], a_fs[1]),
                                                    _pair_lanes(a_fs[2], a_fs[3])], axis=1))
                scale_b = jnp.exp2(jnp.concatenate([_pair_lanes(a_bs[0], a_bs[1]),
                                                    _pair_lanes(a_bs[2], a_bs[3])], axis=1))
                scols = slice(half * 256, (half + 1) * 256)
                s_prev = s_ref[g, :, scols]
                y_off_f = jnp.dot(c_g, s_prev.astype(BF16),
                                  preferred_element_type=F32) * scale_f
                y_off_b = jnp.dot(c_g, sbin_ref[0, cc, g, :, scols],
                                  preferred_element_type=F32) * scale_b
                y_ref[0, rows, cols] = (r[0:q] + y_off_f + y_off_b).astype(BF16)
                s_ref[g, :, scols] = s_prev * scale_f[q - 1:q, :] + r[q:]


def _ssd_main(xsbc3, dt3, dtT, sbin, bias2, alog2, biasT, alogT):
    b, l, _ = xsbc3.shape
    ns = l // (CHUNK * SSD_CPS)
    tq = SSD_CPS * CHUNK
    fwd = lambda bi, i: (bi, i, 0)
    return pl.pallas_call(
        _ssd_main_body,
        grid=(b, ns),
        in_specs=[
            pl.BlockSpec((1, tq, CONV_DIM), fwd),
            pl.BlockSpec((1, tq, 128), fwd),
            pl.BlockSpec((2 * SSM_HEADS, tq), lambda bi, i: (0, bi * ns + i)),
            pl.BlockSpec((1, SSD_CPS, N_GROUPS, D_STATE, GROUP_W),
                         lambda bi, i: (bi, i, 0, 0, 0)),
            _resident((1, 128)),
            _resident((1, 128)),
            _resident((2 * SSM_HEADS, 1)),
            _resident((2 * SSM_HEADS, 1)),
        ],
        out_specs=pl.BlockSpec((1, tq, D_INNER), fwd),
        out_shape=jax.ShapeDtypeStruct((b, l, D_INNER), BF16),
        scratch_shapes=[pltpu.VMEM((N_GROUPS, D_STATE, GROUP_W), F32)],
        compiler_params=pltpu.CompilerParams(
            dimension_semantics=("parallel", "arbitrary"), vmem_limit_bytes=VMEM_LIMIT_BYTES),
        name="ssd_main",
    )(xsbc3, dt3, dtT, sbin, bias2, alog2, biasT, alogT)


def _out_body(x_ref, attn_ref, y_ref, xs_ref, z_ref, gate_ref, dsk_ref, nrm_ref,
              wap_ref, wsp_ref, wo_ref, o_ref):
    z = z_ref[...].astype(F32)
    y = (y_ref[...].astype(F32) + dsk_ref[...] * xs_ref[...].astype(F32)) * (z * _sigmoid(z))
    parts = []
    for g in range(N_GROUPS):
        cols = slice(g * GROUP_W, (g + 1) * GROUP_W)
        parts.append(_rms_normed(y[:, cols], nrm_ref[:, cols]).astype(BF16))
    ssm = jnp.concatenate(parts, axis=1)
    g_attn = gate_ref[:, :D_MODEL].astype(F32)
    g_ssm = gate_ref[:, D_MODEL:].astype(F32)
    merged = (_sigmoid(g_attn) * jnp.dot(attn_ref[...], wap_ref[...], preferred_element_type=F32)
              + _sigmoid(g_ssm) * jnp.dot(ssm, wsp_ref[...], preferred_element_type=F32))
    o_ref[...] = x_ref[...] + jnp.dot(merged.astype(BF16), wo_ref[...],
                                      preferred_element_type=F32)


def _merge_out(x2, attn2, y2, xsbc2, z2, gate2, dsk, nrm, wap, wsp, wo):
    t = x2.shape[0]
    tm = OUT_TM
    row = lambda w: pl.BlockSpec((tm, w), lambda i: (i, 0))
    return pl.pallas_call(
        _out_body,
        grid=(t // tm,),
        in_specs=[
            row(D_MODEL), row(1024), row(D_INNER), row(D_INNER), row(D_INNER),
            row(2 * D_MODEL),
            _resident((1, D_INNER)), _resident((1, D_INNER)),
            _resident(wap.shape), _resident(wsp.shape), _resident(wo.shape),
        ],
        out_specs=row(D_MODEL),
        out_shape=jax.ShapeDtypeStruct((t, D_MODEL), F32),
        compiler_params=pltpu.CompilerParams(
            dimension_semantics=("parallel",), vmem_limit_bytes=VMEM_LIMIT_BYTES),
        name="merge_out",
    )(x2, attn2, y2, xsbc2, z2, gate2, dsk, nrm, wap, wsp, wo)


def _ffn_weights(w_up, w_down):
    nck = D_FF // FFN_CK
    a = w_up[:, :D_FF].reshape(D_MODEL, nck, FFN_CK)
    b = w_up[:, D_FF:].reshape(D_MODEL, nck, FFN_CK)
    wu_r = jnp.concatenate([a, b], axis=-1).reshape(D_MODEL, 2 * D_FF).astype(BF16)
    return wu_r, w_down.astype(BF16)


def _layer_weights(layer, ln_ffn1, w_ffn1_up, w_ffn1_down, ln_mix, w_in, q_norm, k_norm, rel_bias,
                   conv_w, conv_b, dt_bias_fwd, dt_bias_bwd, a_log_fwd, a_log_bwd, d_skip,
                   ssm_norm, w_attn_proj, w_ssm_proj, w_out, ln_ffn2, w_ffn2_up, w_ffn2_down):
    w = {}
    w["ln1"] = ln_ffn1[layer].reshape(1, D_MODEL)
    w["ffn1"] = _ffn_weights(w_ffn1_up[layer], w_ffn1_down[layer])
    w["ln2"] = ln_ffn2[layer].reshape(1, D_MODEL)
    w["ffn2"] = _ffn_weights(w_ffn2_up[layer], w_ffn2_down[layer])
    w["ln_mix"] = ln_mix[layer].reshape(1, D_MODEL)
    wi = w_in[layer]
    dt0 = 3 * D_MODEL + D_INNER + CONV_DIM
    xbc0 = 3 * D_MODEL + D_INNER
    w["w_a"] = jnp.concatenate([wi[:, :xbc0], wi[:, dt0 + 2 * SSM_HEADS:]], axis=1).astype(BF16)
    w["w_x"] = wi[:, xbc0:dt0].astype(BF16)
    wdt = wi[:, dt0:dt0 + 2 * SSM_HEADS]
    w["w_dt"] = jnp.concatenate([wdt, wdt], axis=1).astype(BF16)
    w["w_dtT"] = jnp.transpose(wdt).astype(BF16)
    w["qg"] = (jnp.tile(q_norm[layer], N_HEADS)
               * (HEAD_DIM ** -0.5 * LOG2_E)).reshape(1, D_MODEL)
    w["kg"] = jnp.tile(k_norm[layer], N_HEADS).reshape(1, D_MODEL)
    w["bias_tab"] = _attn_bias_table(rel_bias[layer])
    w["conv_w"] = jnp.concatenate(
        [conv_w[layer], jnp.zeros((8 - CONV_W, CONV_DIM), F32)], axis=0)
    w["conv_b"] = conv_b[layer].reshape(1, CONV_DIM)
    bias = jnp.concatenate([dt_bias_fwd[layer], dt_bias_bwd[layer]])
    alog = jnp.concatenate([a_log_fwd[layer], a_log_bwd[layer]])
    w["bias2"] = jnp.tile(bias, 2).reshape(1, 128)
    w["alog2"] = jnp.tile(alog, 2).reshape(1, 128)
    w["biasT"] = bias.reshape(2 * SSM_HEADS, 1)
    w["alogT"] = alog.reshape(2 * SSM_HEADS, 1)
    w["dsk"] = jnp.repeat(d_skip[layer], SSM_HEAD_DIM).reshape(1, D_INNER)
    w["nrm"] = ssm_norm[layer].reshape(1, D_INNER)
    w["wap"] = w_attn_proj[layer].astype(BF16)
    w["wsp"] = w_ssm_proj[layer].astype(BF16)
    w["wo"] = w_out[layer].astype(BF16)
    return w


def _head_mean_matrix():
    r = jnp.arange(D_MODEL)[:, None] // HEAD_DIM
    c = jnp.arange(D_MODEL)[None, :] // HEAD_DIM
    return jnp.where(r == c, 1.0 / HEAD_DIM, 0.0).astype(BF16)


def _encoder_layer(x2, b, l, w, pmat):
    t = b * l
    x2 = _ffn(x2, w["ln1"], *w["ffn1"])
    q, k, v, z, gate, dt, dt_t, xsbc = _inproj(
        x2, l, w["ln_mix"], w["w_a"], w["w_x"], w["w_dt"], w["w_dtT"], w["qg"], w["kg"], pmat,
        w["conv_w"], w["conv_b"])
    attn = _attention(q.reshape(b, l, -1), k.reshape(b, l, -1), v.reshape(b, l, -1),
                      w["bias_tab"])
    xsbc3 = xsbc.reshape(b, l, CONV_DIM)
    dt3 = dt.reshape(b, l, 128)
    scalars = (w["bias2"], w["alog2"], w["biasT"], w["alogT"])
    s_bwd_in = _ssd_bstate(xsbc3, dt3, dt_t, *scalars)
    y = _ssd_main(xsbc3, dt3, dt_t, s_bwd_in, *scalars)
    x2 = _merge_out(x2, attn.reshape(t, -1), y.reshape(t, D_INNER), xsbc, z, gate,
                    w["dsk"], w["nrm"], w["wap"], w["wsp"], w["wo"])
    return _ffn(x2, w["ln2"], *w["ffn2"])


def kernel(x_prompt, x_sample, ln_ffn1, w_ffn1_up, w_ffn1_down, ln_mix, w_in, q_norm, k_norm,
           rel_bias, conv_w, conv_b, dt_bias_fwd, dt_bias_bwd, a_log_fwd, a_log_bwd, d_skip,
           ssm_norm, w_attn_proj, w_ssm_proj, w_out, ln_ffn2, w_ffn2_up, w_ffn2_down):
    params = (ln_ffn1, w_ffn1_up, w_ffn1_down, ln_mix, w_in, q_norm, k_norm, rel_bias,
              conv_w, conv_b, dt_bias_fwd, dt_bias_bwd, a_log_fwd, a_log_bwd, d_skip,
              ssm_norm, w_attn_proj, w_ssm_proj, w_out, ln_ffn2, w_ffn2_up, w_ffn2_down)
    depth = ln_ffn1.shape[0]
    layers = [_layer_weights(i, *params) for i in range(depth)]
    pmat = _head_mean_matrix()

    def run_trunk(x):
        b, l, d = x.shape
        x2 = x.reshape(b * l, d)
        for w in layers:
            x2 = _encoder_layer(x2, b, l, w, pmat)
        return x2.reshape(b, l, d)

    return (run_trunk(x_prompt), run_trunk(x_sample))
```

```python
import functools

import jax
import jax.numpy as jnp
import numpy as np
from jax import lax
from jax.experimental import pallas as pl
from jax.experimental.pallas import tpu as pltpu

F32 = jnp.float32
BF16 = jnp.bfloat16

D_MODEL = 1024
GRID_W = 64
WIN_R = 8
WIN_C = 16
N_HEADS = 16
HEAD_DIM = 64
D_INNER = 2048
SSM_HEADS = 32
SSM_HEAD_DIM = 64
N_GROUPS = 4
D_STATE = 128
CONV_W = 5
CONV_DIM = D_INNER + 2 * N_GROUPS * D_STATE
CHUNK = 128
D_FF = 2816
RMS_EPS = 1e-6
NEG_BIG = -1e30
LOG2_E = 1.4426950408889634

VMEM_LIMIT_BYTES = 56 * 1024 * 1024

FFN_TM = 1024
FFN_CK = 256
INP_TM = 256
CONV_HALO = 16
SSD_CPS = 4
ATT_QR = 32
ATT_HG = 4
OUT_TM = 512
GROUP_W = D_INNER // N_GROUPS


def _resident(shape):
    nd = len(shape)
    return pl.BlockSpec(shape, lambda *_: (0,) * nd, pipeline_mode=pl.Buffered(1))


def _sigmoid(x):
    return 0.5 * jnp.tanh(0.5 * x) + 0.5


def _softplus(x):
    return jnp.maximum(x, 0.0) + jnp.log(1.0 + jnp.exp(-jnp.abs(x)))


def _rms_normed(x, gain):
    ms = jnp.mean(x * x, axis=-1, keepdims=True)
    return x * lax.rsqrt(ms + RMS_EPS) * gain


def _ffn_residual(x, g_ref, wu_ref, wd_ref):
    xn = _rms_normed(x, g_ref[...]).astype(BF16)
    acc = jnp.zeros(x.shape, F32)
    for c in range(D_FF // FFN_CK):
        ab = jnp.dot(xn, wu_ref[:, c * 2 * FFN_CK:(c + 1) * 2 * FFN_CK],
                     preferred_element_type=F32)
        a = ab[:, :FFN_CK]
        b = ab[:, FFN_CK:]
        h = (a * _sigmoid(a) * b).astype(BF16)
        acc = acc + jnp.dot(h, wd_ref[c * FFN_CK:(c + 1) * FFN_CK, :],
                            preferred_element_type=F32)
    return x + 0.5 * acc


def _ffn_body(x_ref, g_ref, wu_ref, wd_ref, o_ref):
    o_ref[...] = _ffn_residual(x_ref[...], g_ref, wu_ref, wd_ref)


def _ffn(x2, gain, wu_r, wd):
    t = x2.shape[0]
    return pl.pallas_call(
        _ffn_body,
        grid=(t // FFN_TM,),
        in_specs=[
            pl.BlockSpec((FFN_TM, D_MODEL), lambda i: (i, 0)),
            _resident((1, D_MODEL)),
            _resident((D_MODEL, 2 * D_FF)),
            _resident((D_FF, D_MODEL)),
        ],
        out_specs=pl.BlockSpec((FFN_TM, D_MODEL), lambda i: (i, 0)),
        out_shape=jax.ShapeDtypeStruct((t, D_MODEL), F32),
        compiler_params=pltpu.CompilerParams(
            dimension_semantics=("parallel",), vmem_limit_bytes=VMEM_LIMIT_BYTES),
        name="ffn",
    )(x2, gain, wu_r, wd)


def _inproj_body(x_ref, g_ref, w_ref, wx_ref, wdt_ref, wdtT_ref, qg_ref, kg_ref, p_ref,
                 cw_ref, cb_ref,
                 q_ref, k_ref, v_ref, z_ref, gate_ref, dt_ref, dtT_ref, xsbc_ref,
                 cur_ref, ext_ref, *, n_tiles, tiles_per_seq):
    i = pl.program_id(0)
    tm = INP_TM
    halo = CONV_HALO

    @pl.when(i == 0)
    def _():
        ext_ref[...] = jnp.zeros(ext_ref.shape, F32)

    base = halo - CONV_W // 2

    conv_cw = 512

    def conv_rows(r0, nrows, c):
        win = nrows + 2 * halo
        g0, ng = halo // 8, nrows // 8
        sub = lax.broadcasted_iota(jnp.int32, (ng, 8, conv_cw), 1)
        cols = slice(c * conv_cw, (c + 1) * conv_cw)
        e = ext_ref[r0:r0 + win, cols].reshape(win // 8, 8, conv_cw)
        acc = e[g0:g0 + ng] * cw_ref[CONV_W // 2:CONV_W // 2 + 1, cols]
        for j in range(CONV_W):
            d = j - CONV_W // 2
            if d == 0:
                continue
            rot = pltpu.roll(e, (-d) % 8, 1)
            if d > 0:
                tap = jnp.where(sub < 8 - d, rot[g0:g0 + ng], rot[g0 + 1:g0 + ng + 1])
            else:
                tap = jnp.where(sub >= -d, rot[g0:g0 + ng], rot[g0 - 1:g0 + ng - 1])
            acc = acc + tap * cw_ref[j:j + 1, cols]
        acc = acc + cb_ref[:, cols]
        y = (acc * _sigmoid(acc)).reshape(nrows, conv_cw)
        xsbc_ref[r0:r0 + nrows, cols] = y.astype(BF16)

    h = _rms_normed(x_ref[...], g_ref[...]).astype(BF16)

    def proj(lo, hi):
        return jnp.dot(h, w_ref[:, lo:hi], preferred_element_type=F32)

    def head_normed(r, gain):
        ms = jnp.dot((r * r).astype(BF16), p_ref[...], preferred_element_type=F32)
        return (r * lax.rsqrt(ms + RMS_EPS) * gain).astype(BF16)

    def store_q():
        q_ref[...] = head_normed(proj(0, 1024), qg_ref[...])

    def store_k():
        k_ref[...] = head_normed(proj(1024, 2048), kg_ref[...])

    def store_v():
        v_ref[...] = proj(2048, 3072).astype(BF16)

    def store_z(c):
        z_ref[:, c * 1024:(c + 1) * 1024] = proj(3072 + c * 1024, 4096 + c * 1024).astype(BF16)

    def store_gate(c):
        gate_ref[:, c * 1024:(c + 1) * 1024] = proj(5120 + c * 1024,
                                                    6144 + c * 1024).astype(BF16)

    mxu_work = [store_q, store_k, store_v, lambda: store_z(0), lambda: store_z(1),
                lambda: store_gate(0)]
    n_conv = CONV_DIM // conv_cw
    per_mxu = n_conv // len(mxu_work)
    for c in range(n_conv):
        conv_rows(0, tm - halo, c)
        if (c + 1) % per_mxu == 0:
            mxu_work[c // per_mxu]()

    for c in range(3):
        cur_ref[:, c * 1024:(c + 1) * 1024] = jnp.dot(
            h, wx_ref[:, c * 1024:(c + 1) * 1024], preferred_element_type=F32)
    has_next = jnp.logical_and(i < n_tiles, (i - 1) % tiles_per_seq != tiles_per_seq - 1)
    ext_ref[halo + tm:, :] = jnp.where(has_next, cur_ref[0:halo, :], 0.0)
    for c in range(CONV_DIM // conv_cw):
        conv_rows(tm - halo, halo, c)
    store_gate(1)

    dt_ref[...] = jnp.dot(h, wdt_ref[...], preferred_element_type=F32)
    dtT_ref[...] = lax.dot_general(wdtT_ref[...], h, (((1,), (1,)), ((), ())),
                                   preferred_element_type=F32)

    has_prev = i % tiles_per_seq != 0
    tail = ext_ref[tm:tm + halo, :]
    ext_ref[halo:halo + tm, :] = cur_ref[...]
    ext_ref[0:halo, :] = jnp.where(has_prev, tail, 0.0)


def _inproj(x2, seq_len, gain, w_a, w_x, w_dt, w_dtT, qg, kg, pmat, conv_w, conv_b):
    t = x2.shape[0]
    tm = INP_TM
    nt = t // tm
    cur = lambda w: pl.BlockSpec((tm, w), lambda i: (jnp.minimum(i, nt - 1), 0))
    return pl.pallas_call(
        functools.partial(_inproj_body, n_tiles=nt, tiles_per_seq=seq_len // tm),
        grid=(nt + 1,),
        in_specs=[
            cur(D_MODEL),
            _resident((1, D_MODEL)),
            _resident(w_a.shape),
            _resident(w_x.shape),
            _resident(w_dt.shape),
            _resident(w_dtT.shape),
            _resident((1, D_MODEL)),
            _resident((1, D_MODEL)),
            _resident((D_MODEL, D_MODEL)),
            _resident((8, CONV_DIM)),
            _resident((1, CONV_DIM)),
        ],
        out_specs=[cur(1024), cur(1024), cur(1024), cur(2048), cur(2048), cur(128),
                   pl.BlockSpec((2 * SSM_HEADS, tm), lambda i: (0, jnp.minimum(i, nt - 1))),
                   pl.BlockSpec((tm, CONV_DIM), lambda i: (jnp.maximum(i - 1, 0), 0))],
        out_shape=[
            jax.ShapeDtypeStruct((t, 1024), BF16),
            jax.ShapeDtypeStruct((t, 1024), BF16),
            jax.ShapeDtypeStruct((t, 1024), BF16),
            jax.ShapeDtypeStruct((t, 2048), BF16),
            jax.ShapeDtypeStruct((t, 2048), BF16),
            jax.ShapeDtypeStruct((t, 128), F32),
            jax.ShapeDtypeStruct((2 * SSM_HEADS, t), F32),
            jax.ShapeDtypeStruct((t, CONV_DIM), BF16),
        ],
        scratch_shapes=[pltpu.VMEM((tm, CONV_DIM), F32),
                        pltpu.VMEM((tm + 2 * CONV_HALO, CONV_DIM), F32)],
        compiler_params=pltpu.CompilerParams(
            dimension_semantics=("arbitrary",), vmem_limit_bytes=VMEM_LIMIT_BYTES),
        name="inproj",
    )(x2, gain, w_a, w_x, w_dt, w_dtT, qg, kg, pmat, conv_w, conv_b)


def _attn_body(q_ref, k_ref, v_ref, bias_ref, o_ref, *, rows):
    j = pl.program_id(2)
    cw = ATT_HG * HEAD_DIM
    head_of_lane = lax.broadcasted_iota(jnp.int32, (GRID_W, cw), 1) // HEAD_DIM
    nk = WIN_R * GRID_W

    def row_body(i, carry):
        r = j * ATT_QR + i
        r0 = jnp.clip(r - WIN_R // 2, 0, rows - WIN_R)
        cls = r - r0
        qs = pl.multiple_of(i * GRID_W, GRID_W)
        ks = pl.multiple_of(r0 * GRID_W, GRID_W)
        q_row = q_ref[0, pl.ds(qs, GRID_W), :]
        k_win = k_ref[0, pl.ds(ks, nk), :]
        v_win = v_ref[0, pl.ds(ks, nk), :]
        q_bd = jnp.concatenate(
            [jnp.where(head_of_lane == h, q_row, jnp.zeros_like(q_row)) for h in range(ATT_HG)],
            axis=0)
        s = lax.dot_general(q_bd, k_win, (((1,), (1,)), ((), ())),
                            preferred_element_type=F32)
        s = s + bias_ref[cls].reshape(ATT_HG * GRID_W, nk)
        m = jnp.max(s, axis=-1, keepdims=True)
        p = jnp.exp2(s - m)
        inv = 1.0 / jnp.sum(p, axis=-1, keepdims=True)
        o = jnp.dot(p.astype(BF16), v_win, preferred_element_type=F32) * inv
        out = o[(ATT_HG - 1) * GRID_W:]
        for h in range(ATT_HG - 2, -1, -1):
            out = jnp.where(head_of_lane == h, o[h * GRID_W:(h + 1) * GRID_W], out)
        o_ref[0, pl.ds(qs, GRID_W), :] = out.astype(BF16)
        return carry

    lax.fori_loop(0, ATT_QR, row_body, 0, unroll=True)


def _attention(q3, k3, v3, bias_tab):
    b, l, _ = q3.shape
    rows = l // GRID_W
    assert rows >= WIN_R and rows % ATT_QR == 0
    tq = ATT_QR * GRID_W
    cw = ATT_HG * HEAD_DIM
    return pl.pallas_call(
        functools.partial(_attn_body, rows=rows),
        grid=(b, N_HEADS // ATT_HG, rows // ATT_QR),
        in_specs=[
            pl.BlockSpec((1, tq, cw), lambda bi, g, j: (bi, j, g)),
            pl.BlockSpec((1, l, cw), lambda bi, g, j: (bi, 0, g)),
            pl.BlockSpec((1, l, cw), lambda bi, g, j: (bi, 0, g)),
            pl.BlockSpec((WIN_R, ATT_HG, GRID_W, WIN_R * GRID_W), lambda bi, g, j: (0, g, 0, 0)),
        ],
        out_specs=pl.BlockSpec((1, tq, cw), lambda bi, g, j: (bi, j, g)),
        out_shape=jax.ShapeDtypeStruct((b, l, N_HEADS * HEAD_DIM), BF16),
        compiler_params=pltpu.CompilerParams(
            dimension_semantics=("parallel", "parallel", "arbitrary"),
            vmem_limit_bytes=VMEM_LIMIT_BYTES),
        name="natten",
    )(q3, k3, v3, bias_tab)


def _attn_bias_table(rel_bias):
    cols = np.arange(GRID_W)
    col_start = np.clip(cols - WIN_C // 2, 0, GRID_W - WIN_C)
    col_valid = (cols[None, :] >= col_start[:, None]) & (cols[None, :] < col_start[:, None] + WIN_C)
    col_idx = np.clip(cols[None, :] - cols[:, None] + WIN_C - 1, 0, 2 * WIN_C - 2)
    onehot = (np.arange(2 * WIN_C - 1)[:, None, None] == col_idx[None]).astype(np.float32)
    full = jnp.einsum("hrc,cqk->hqrk", rel_bias.astype(F32), jnp.asarray(onehot),
                      precision=lax.Precision.HIGHEST)
    full = jnp.where(jnp.asarray(col_valid)[None, :, None, :], full * LOG2_E, NEG_BIG)
    full = full.reshape(N_HEADS, GRID_W, (2 * WIN_R - 1) * GRID_W)
    nk = WIN_R * GRID_W
    return jnp.stack([full[:, :, (WIN_R - 1 - c) * GRID_W:(WIN_R - 1 - c) * GRID_W + nk]
                      for c in range(WIN_R)], axis=0)


def _split3(v):
    hi = v.astype(BF16)
    r1 = v - hi.astype(F32)
    mid = r1.astype(BF16)
    lo = (r1 - mid.astype(F32)).astype(BF16)
    return hi, mid, lo


def _tri_masks():
    q = CHUNK
    ri = lax.broadcasted_iota(jnp.int32, (q, q), 0)
    ci = lax.broadcasted_iota(jnp.int32, (q, q), 1)
    return ri, ci


def _row_major_scalars(dt_raw_t, biasT_ref, alogT_ref):
    q = CHUNK
    ri, ci = _tri_masks()
    lower_b = (ri >= ci).astype(BF16)
    upper_b = (ri <= ci).astype(BF16)
    fwd_row = lax.broadcasted_iota(jnp.int32, (2 * SSM_HEADS, 1), 0) < SSM_HEADS
    dt_t = _softplus(dt_raw_t + biasT_ref[...])
    adt_t = dt_t * (-jnp.exp(alogT_ref[...]))
    pieces_t = jnp.concatenate(_split3(adt_t), axis=1)
    acum_t = jnp.where(
        fwd_row,
        jnp.dot(pieces_t, jnp.concatenate([upper_b] * 3, axis=0), preferred_element_type=F32),
        jnp.dot(pieces_t, jnp.concatenate([lower_b] * 3, axis=0), preferred_element_type=F32))
    last_t = jnp.where(fwd_row, acum_t[:, q - 1:q], acum_t[:, 0:1])
    g_t = acum_t - jnp.log(dt_t)
    w_t = dt_t * jnp.exp(last_t - acum_t)
    return dt_t, acum_t, g_t, w_t


def _block_diag4(x4):
    q = CHUNK
    shape = (4 * q, 4 * SSM_HEAD_DIM)
    keep = (lax.broadcasted_iota(jnp.int32, shape, 0) // q
            == lax.broadcasted_iota(jnp.int32, shape, 1) // SSM_HEAD_DIM)
    return jnp.where(keep, jnp.concatenate([x4] * 4, axis=0), jnp.zeros((), x4.dtype))


def _pair_lanes(a, b):
    low = lax.broadcasted_iota(jnp.int32, a.shape, 1) < SSM_HEAD_DIM
    return jnp.where(low, a, b)


def _ssd_bstate_body(xs_ref, b_ref, dt_ref, dtT_ref, bias2_ref, alog2_ref, biasT_ref, alogT_ref,
                     sbin_ref, s_ref):
    @pl.when(pl.program_id(1) == 0)
    def _():
        s_ref[...] = jnp.zeros(s_ref.shape, F32)

    q = CHUNK
    for cc in range(SSD_CPS - 1, -1, -1):
        rows = slice(cc * q, (cc + 1) * q)
        _, _, _, w_t = _row_major_scalars(dtT_ref[:, rows], biasT_ref, alogT_ref)
        w_t16 = w_t.astype(BF16)
        adt = _softplus(dt_ref[0, rows, :] + bias2_ref[...]) * (-jnp.exp(alog2_ref[...]))
        chunk_decay = jnp.broadcast_to(jnp.exp(jnp.sum(adt, axis=0, keepdims=True)), (8, 128))

        for g in range(N_GROUPS):
            b_t16 = jnp.transpose(
                b_ref[0, rows, g * D_STATE:(g + 1) * D_STATE].astype(F32)).astype(BF16)
            for half in range(2):
                heads = [SSM_HEADS + g * 8 + half * 4 + hh for hh in range(4)]
                lhs = jnp.concatenate([b_t16 * w_t16[h:h + 1, :] for h in heads], axis=1)
                cols = slice(g * GROUP_W + half * 256, g * GROUP_W + (half + 1) * 256)
                new = jnp.dot(lhs, _block_diag4(xs_ref[0, rows, cols]),
                              preferred_element_type=F32)
                dec = [jnp.broadcast_to(chunk_decay[:, h:h + 1], (8, 128)) for h in heads]
                decay = jnp.concatenate(
                    [_pair_lanes(dec[0], dec[1]), _pair_lanes(dec[2], dec[3])],
                    axis=1)[0:1, :]
                scols = slice(half * 256, (half + 1) * 256)
                s_prev = s_ref[g, :, scols]
                sbin_ref[0, cc, g, :, scols] = s_prev.astype(BF16)
                s_ref[g, :, scols] = s_prev * decay + new


def _ssd_bstate(xsbc3, dt3, dtT, bias2, alog2, biasT, alogT):
    b, l, _ = xsbc3.shape
    nc = l // CHUNK
    ns = nc // SSD_CPS
    tq = SSD_CPS * CHUNK
    rev = lambda bi, i: (bi, ns - 1 - i, 0)
    return pl.pallas_call(
        _ssd_bstate_body,
        grid=(b, ns),
        in_specs=[
            pl.BlockSpec((1, tq, D_INNER), rev),
            pl.BlockSpec((1, tq, N_GROUPS * D_STATE),
                         lambda bi, i: (bi, ns - 1 - i, D_INNER // (N_GROUPS * D_STATE))),
            pl.BlockSpec((1, tq, 128), rev),
            pl.BlockSpec((2 * SSM_HEADS, tq), lambda bi, i: (0, bi * ns + ns - 1 - i)),
            _resident((1, 128)),
            _resident((1, 128)),
            _resident((2 * SSM_HEADS, 1)),
            _resident((2 * SSM_HEADS, 1)),
        ],
        out_specs=pl.BlockSpec((1, SSD_CPS, N_GROUPS, D_STATE, GROUP_W),
                               lambda bi, i: (bi, ns - 1 - i, 0, 0, 0)),
        out_shape=jax.ShapeDtypeStruct((b, nc, N_GROUPS, D_STATE, GROUP_W), BF16),
        scratch_shapes=[pltpu.VMEM((N_GROUPS, D_STATE, GROUP_W), F32)],
        compiler_params=pltpu.CompilerParams(
            dimension_semantics=("parallel", "arbitrary"), vmem_limit_bytes=VMEM_LIMIT_BYTES),
        name="ssd_bstate",
    )(xsbc3, xsbc3, dt3, dtT, bias2, alog2, biasT, alogT)


def _ssd_main_body(x_ref, dt_ref, dtT_ref, sbin_ref, bias2_ref, alog2_ref, biasT_ref, alogT_ref,
                   y_ref, s_ref):
    @pl.when(pl.program_id(1) == 0)
    def _():
        s_ref[...] = jnp.zeros(s_ref.shape, F32)

    q = CHUNK
    ri, ci = _tri_masks()
    below = ri > ci
    on_diag = ri == ci
    lower_b = (ri >= ci).astype(BF16)
    upper_b = (ri <= ci).astype(BF16)

    lane = lax.broadcasted_iota(jnp.int32, (1, 128), 1)
    fwd_col = (lane % 64) < SSM_HEADS

    for cc in range(SSD_CPS):
        rows = slice(cc * q, (cc + 1) * q)
        adt = _softplus(dt_ref[0, rows, :] + bias2_ref[...]) * (-jnp.exp(alog2_ref[...]))
        pieces = jnp.concatenate(_split3(adt), axis=0)
        acum = jnp.where(
            fwd_col,
            jnp.dot(jnp.concatenate([lower_b] * 3, axis=1), pieces, preferred_element_type=F32),
            jnp.dot(jnp.concatenate([upper_b] * 3, axis=1), pieces, preferred_element_type=F32))

        dt_t, _, g_t, w_t = _row_major_scalars(dtT_ref[:, rows], biasT_ref, alogT_ref)
        dt_sum_t = dt_t[0:SSM_HEADS, :] + dt_t[SSM_HEADS:, :]
        acum = acum * LOG2_E
        g_t = g_t * LOG2_E
        w_t16 = w_t.astype(BF16)

        for g in range(N_GROUPS):
            b_g = x_ref[0, rows, D_INNER + g * D_STATE:D_INNER + (g + 1) * D_STATE]
            c_g = x_ref[0, rows, D_INNER + (N_GROUPS + g) * D_STATE:
                        D_INNER + (N_GROUPS + g + 1) * D_STATE]
            cb = lax.dot_general(c_g, b_g, (((1,), (1,)), ((), ())),
                                 preferred_element_type=F32)
            cb16 = cb.astype(BF16)
            b_t16 = jnp.transpose(b_g.astype(F32)).astype(BF16)
            for half in range(2):
                ms, bws, a_fs, a_bs = [], [], [], []
                for hh in range(4):
                    h = g * 8 + half * 4 + hh
                    a_f = jnp.broadcast_to(acum[:, h:h + 1], (q, q))
                    a_b = jnp.broadcast_to(acum[:, SSM_HEADS + h:SSM_HEADS + h + 1], (q, q))
                    arg = jnp.where(below, a_f - g_t[h:h + 1, :],
                                    a_b - g_t[SSM_HEADS + h:SSM_HEADS + h + 1, :])
                    decay = jnp.where(on_diag, dt_sum_t[h:h + 1, :], jnp.exp2(arg))
                    ms.append(cb16 * decay.astype(BF16))
                    bws.append(b_t16 * w_t16[h:h + 1, :])
                    a_fs.append(a_f)
                    a_bs.append(a_b)
                lhs = jnp.concatenate(
                    [jnp.concatenate(ms, axis=1), jnp.concatenate(bws, axis=1)], axis=0)
                cols = slice(g * GROUP_W + half * 256, g * GROUP_W + (half + 1) * 256)
                r = jnp.dot(lhs, _block_diag4(x_ref[0, rows, cols]),
                            preferred_element_type=F32)
                scale_f = jnp.exp2(jnp.concatenate([_pair_lanes(a_fs[0], a_fs[1]),
                                                    _pair_lanes(a_fs[2], a_fs[3])], axis=1))
                scale_b = jnp.exp2(jnp.concatenate([_pair_lanes(a_bs[0], a_bs[1]),
                                                    _pair_lanes(a_bs[2], a_bs[3])], axis=1))
                scols = slice(half * 256, (half + 1) * 256)
                s_prev = s_ref[g, :, scols]
                y_off_f = jnp.dot(c_g, s_prev.astype(BF16),
                                  preferred_element_type=F32) * scale_f
                y_off_b = jnp.dot(c_g, sbin_ref[0, cc, g, :, scols],
                                  preferred_element_type=F32) * scale_b
                y_ref[0, rows, cols] = (r[0:q] + y_off_f + y_off_b).astype(BF16)
                s_ref[g, :, scols] = s_prev * scale_f[q - 1:q, :] + r[q:]


def _ssd_main(xsbc3, dt3, dtT, sbin, bias2, alog2, biasT, alogT):
    b, l, _ = xsbc3.shape
    ns = l // (CHUNK * SSD_CPS)
    tq = SSD_CPS * CHUNK
    fwd = lambda bi, i: (bi, i, 0)
    return pl.pallas_call(
        _ssd_main_body,
        grid=(b, ns),
        in_specs=[
            pl.BlockSpec((1, tq, CONV_DIM), fwd),
            pl.BlockSpec((1, tq, 128), fwd),
            pl.BlockSpec((2 * SSM_HEADS, tq), lambda bi, i: (0, bi * ns + i)),
            pl.BlockSpec((1, SSD_CPS, N_GROUPS, D_STATE, GROUP_W),
                         lambda bi, i: (bi, i, 0, 0, 0)),
            _resident((1, 128)),
            _resident((1, 128)),
            _resident((2 * SSM_HEADS, 1)),
            _resident((2 * SSM_HEADS, 1)),
        ],
        out_specs=pl.BlockSpec((1, tq, D_INNER), fwd),
        out_shape=jax.ShapeDtypeStruct((b, l, D_INNER), BF16),
        scratch_shapes=[pltpu.VMEM((N_GROUPS, D_STATE, GROUP_W), F32)],
        compiler_params=pltpu.CompilerParams(
            dimension_semantics=("parallel", "arbitrary"), vmem_limit_bytes=VMEM_LIMIT_BYTES),
        name="ssd_main",
    )(xsbc3, dt3, dtT, sbin, bias2, alog2, biasT, alogT)


def _out_body(x_ref, attn_ref, y_ref, xs_ref, z_ref, gate_ref, dsk_ref, nrm_ref,
              wap_ref, wsp_ref, wo_ref, o_ref):
    z = z_ref[...].astype(F32)
    y = (y_ref[...].astype(F32) + dsk_ref[...] * xs_ref[...].astype(F32)) * (z * _sigmoid(z))
    parts = []
    for g in range(N_GROUPS):
        cols = slice(g * GROUP_W, (g + 1) * GROUP_W)
        parts.append(_rms_normed(y[:, cols], nrm_ref[:, cols]).astype(BF16))
    ssm = jnp.concatenate(parts, axis=1)
    g_attn = gate_ref[:, :D_MODEL].astype(F32)
    g_ssm = gate_ref[:, D_MODEL:].astype(F32)
    merged = (_sigmoid(g_attn) * jnp.dot(attn_ref[...], wap_ref[...], preferred_element_type=F32)
              + _sigmoid(g_ssm) * jnp.dot(ssm, wsp_ref[...], preferred_element_type=F32))
    o_ref[...] = x_ref[...] + jnp.dot(merged.astype(BF16), wo_ref[...],
                                      preferred_element_type=F32)


def _merge_out(x2, attn2, y2, xsbc2, z2, gate2, dsk, nrm, wap, wsp, wo):
    t = x2.shape[0]
    tm = OUT_TM
    row = lambda w: pl.BlockSpec((tm, w), lambda i: (i, 0))
    return pl.pallas_call(
        _out_body,
        grid=(t // tm,),
        in_specs=[
            row(D_MODEL), row(1024), row(D_INNER), row(D_INNER), row(D_INNER),
            row(2 * D_MODEL),
            _resident((1, D_INNER)), _resident((1, D_INNER)),
            _resident(wap.shape), _resident(wsp.shape), _resident(wo.shape),
        ],
        out_specs=row(D_MODEL),
        out_shape=jax.ShapeDtypeStruct((t, D_MODEL), F32),
        compiler_params=pltpu.CompilerParams(
            dimension_semantics=("parallel",), vmem_limit_bytes=VMEM_LIMIT_BYTES),
        name="merge_out",
    )(x2, attn2, y2, xsbc2, z2, gate2, dsk, nrm, wap, wsp, wo)


def _ffn_weights(w_up, w_down):
    nck = D_FF // FFN_CK
    a = w_up[:, :D_FF].reshape(D_MODEL, nck, FFN_CK)
    b = w_up[:, D_FF:].reshape(D_MODEL, nck, FFN_CK)
    wu_r = jnp.concatenate([a, b], axis=-1).reshape(D_MODEL, 2 * D_FF).astype(BF16)
    return wu_r, w_down.astype(BF16)


def _layer_weights(layer, ln_ffn1, w_ffn1_up, w_ffn1_down, ln_mix, w_in, q_norm, k_norm, rel_bias,
                   conv_w, conv_b, dt_bias_fwd, dt_bias_bwd, a_log_fwd, a_log_bwd, d_skip,
                   ssm_norm, w_attn_proj, w_ssm_proj, w_out, ln_ffn2, w_ffn2_up, w_ffn2_down):
    w = {}
    w["ln1"] = ln_ffn1[layer].reshape(1, D_MODEL)
    w["ffn1"] = _ffn_weights(w_ffn1_up[layer], w_ffn1_down[layer])
    w["ln2"] = ln_ffn2[layer].reshape(1, D_MODEL)
    w["ffn2"] = _ffn_weights(w_ffn2_up[layer], w_ffn2_down[layer])
    w["ln_mix"] = ln_mix[layer].reshape(1, D_MODEL)
    wi = w_in[layer]
    dt0 = 3 * D_MODEL + D_INNER + CONV_DIM
    xbc0 = 3 * D_MODEL + D_INNER
    w["w_a"] = jnp.concatenate([wi[:, :xbc0], wi[:, dt0 + 2 * SSM_HEADS:]], axis=1).astype(BF16)
    w["w_x"] = wi[:, xbc0:dt0].astype(BF16)
    wdt = wi[:, dt0:dt0 + 2 * SSM_HEADS]
    w["w_dt"] = jnp.concatenate([wdt, wdt], axis=1).astype(BF16)
    w["w_dtT"] = jnp.transpose(wdt).astype(BF16)
    w["qg"] = (jnp.tile(q_norm[layer], N_HEADS)
               * (HEAD_DIM ** -0.5 * LOG2_E)).reshape(1, D_MODEL)
    w["kg"] = jnp.tile(k_norm[layer], N_HEADS).reshape(1, D_MODEL)
    w["bias_tab"] = _attn_bias_table(rel_bias[layer])
    w["conv_w"] = jnp.concatenate(
        [conv_w[layer], jnp.zeros((8 - CONV_W, CONV_DIM), F32)], axis=0)
    w["conv_b"] = conv_b[layer].reshape(1, CONV_DIM)
    bias = jnp.concatenate([dt_bias_fwd[layer], dt_bias_bwd[layer]])
    alog = jnp.concatenate([a_log_fwd[layer], a_log_bwd[layer]])
    w["bias2"] = jnp.tile(bias, 2).reshape(1, 128)
    w["alog2"] = jnp.tile(alog, 2).reshape(1, 128)
    w["biasT"] = bias.reshape(2 * SSM_HEADS, 1)
    w["alogT"] = alog.reshape(2 * SSM_HEADS, 1)
    w["dsk"] = jnp.repeat(d_skip[layer], SSM_HEAD_DIM).reshape(1, D_INNER)
    w["nrm"] = ssm_norm[layer].reshape(1, D_INNER)
    w["wap"] = w_attn_proj[layer].astype(BF16)
    w["wsp"] = w_ssm_proj[layer].astype(BF16)
    w["wo"] = w_out[layer].astype(BF16)
    return w


def _head_mean_matrix():
    r = jnp.arange(D_MODEL)[:, None] // HEAD_DIM
    c = jnp.arange(D_MODEL)[None, :] // HEAD_DIM
    return jnp.where(r == c, 1.0 / HEAD_DIM, 0.0).astype(BF16)


def _encoder_layer(x2, b, l, w, pmat):
    t = b * l
    x2 = _ffn(x2, w["ln1"], *w["ffn1"])
    q, k, v, z, gate, dt, dt_t, xsbc = _inproj(
        x2, l, w["ln_mix"], w["w_a"], w["w_x"], w["w_dt"], w["w_dtT"], w["qg"], w["kg"], pmat,
        w["conv_w"], w["conv_b"])
    attn = _attention(q.reshape(b, l, -1), k.reshape(b, l, -1), v.reshape(b, l, -1),
                      w["bias_tab"])
    xsbc3 = xsbc.reshape(b, l, CONV_DIM)
    dt3 = dt.reshape(b, l, 128)
    scalars = (w["bias2"], w["alog2"], w["biasT"], w["alogT"])
    s_bwd_in = _ssd_bstate(xsbc3, dt3, dt_t, *scalars)
    y = _ssd_main(xsbc3, dt3, dt_t, s_bwd_in, *scalars)
    x2 = _merge_out(x2, attn.reshape(t, -1), y.reshape(t, D_INNER), xsbc, z, gate,
                    w["dsk"], w["nrm"], w["wap"], w["wsp"], w["wo"])
    return _ffn(x2, w["ln2"], *w["ffn2"])


def kernel(x_prompt, x_sample, ln_ffn1, w_ffn1_up, w_ffn1_down, ln_mix, w_in, q_norm, k_norm,
           rel_bias, conv_w, conv_b, dt_bias_fwd, dt_bias_bwd, a_log_fwd, a_log_bwd, d_skip,
           ssm_norm, w_attn_proj, w_ssm_proj, w_out, ln_ffn2, w_ffn2_up, w_ffn2_down):
    params = (ln_ffn1, w_ffn1_up, w_ffn1_down, ln_mix, w_in, q_norm, k_norm, rel_bias,
              conv_w, conv_b, dt_bias_fwd, dt_bias_bwd, a_log_fwd, a_log_bwd, d_skip,
              ssm_norm, w_attn_proj, w_ssm_proj, w_out, ln_ffn2, w_ffn2_up, w_ffn2_down)
    depth = ln_ffn1.shape[0]
    layers = [_layer_weights(i, *params) for i in range(depth)]
    pmat = _head_mean_matrix()

    def run_trunk(x):
        b, l, d = x.shape
        x2 = x.reshape(b * l, d)
        for w in layers:
            x2 = _encoder_layer(x2, b, l, w, pmat)
        return x2.reshape(b, l, d)

    return (run_trunk(x_prompt), run_trunk(x_sample))
```
